```python
import jax
import jax.numpy as jnp
from jax import lax
import numpy as np

D_MODEL = 2048
BATCH = 2
SEQ = 16384
DEPTH = 2
DEC_BATCH = 16
DEC_SEQ = 16
PAST_LEN = 1024

CHUNK = 64
D_H = D_MODEL
HG_HEAD_K = 128
HG_HEADS = D_H // HG_HEAD_K
HG_HEAD_V = D_H // HG_HEADS
D_R = D_MODEL
RG_BLOCKS = 16
RG_BLOCK_W = D_R // RG_BLOCKS
CONV_W = 4
RG_C = 8.0
EPS = 1e-6
PROJ = 4 * D_H + 2 * D_R + 2 * D_MODEL
SPLITS = (D_H, 2 * D_H, 3 * D_H, 4 * D_H, 4 * D_H + D_R, 4 * D_H + 2 * D_R, 4 * D_H + 2 * D_R + D_MODEL)
F32 = jnp.float32

kernel_name = 'hybrid_hgrn2_rglru_stream_step'


def rms_norm(x, g):
    xf = x.astype(F32)
    y = xf * lax.rsqrt(jnp.mean(xf * xf, axis=-1, keepdims=True) + EPS)
    return (y * g.astype(F32)).astype(x.dtype)


def gated_recurrence_chunked(q, log_f, k, v, s0, chunk):
    b_sz, t_len, n_h, _ = q.shape
    d_v = v.shape[-1]
    n_c = t_len // chunk

    def to_chunks(a):
        return a.reshape(b_sz, n_c, chunk, n_h, a.shape[-1]).transpose(1, 0, 3, 2, 4)

    causal = jnp.tril(jnp.ones((chunk, chunk), dtype=bool))[None, None, :, :, None]

    def step(s, blk):
        qc, gc, kc, vc = blk
        b = jnp.cumsum(gc, axis=2)
        diff = b[:, :, :, None, :] - b[:, :, None, :, :]
        decay = jnp.exp(jnp.where(causal, diff, -jnp.inf))
        scores = jnp.einsum('bhtk,bhsk,bhtsk->bhts', qc, kc, decay)
        o = (jnp.einsum('bhts,bhsv->bhtv', scores, vc)
             + jnp.einsum('bhtk,bhkv->bhtv', qc * jnp.exp(b), s))
        b_end = b[:, :, -1:, :]
        s_new = (jnp.exp(b_end[:, :, 0, :, None]) * s
                 + jnp.einsum('bhsk,bhsv->bhkv', kc * jnp.exp(b_end - b), vc))
        return s_new, o

    s_fin, o = lax.scan(step, s0, (to_chunks(q), to_chunks(log_f), to_chunks(k), to_chunks(v)))
    o = o.transpose(1, 0, 3, 2, 4).reshape(b_sz, t_len, n_h, d_v)
    return o, s_fin


def hgrn2_mixer(q_raw, f_raw, v_raw, s0, lb, layer, norm_g):
    b_sz, t_len, _ = q_raw.shape
    shp = (b_sz, t_len, HG_HEADS, HG_HEAD_K)
    q = jax.nn.silu(q_raw.astype(F32)).reshape(shp)
    z = f_raw.astype(F32).reshape(shp)
    if layer == 0:
        log_f = jax.nn.log_sigmoid(z)
        k = jax.nn.sigmoid(-z)
    else:
        lbh = lb.reshape(HG_HEADS, HG_HEAD_K)
        log_f = jnp.log(lbh + (1.0 - lbh) * jax.nn.sigmoid(z))
        k = (1.0 - lbh) * jax.nn.sigmoid(-z)
    v = v_raw.astype(F32).reshape(b_sz, t_len, HG_HEADS, HG_HEAD_V)
    o, s_new = gated_recurrence_chunked(q, log_f, k, v, s0.astype(F32), min(CHUNK, t_len))
    o = o * lax.rsqrt(jnp.mean(o * o, axis=-1, keepdims=True) + EPS)
    o = o * norm_g.astype(F32).reshape(HG_HEADS, HG_HEAD_V)
    return o.reshape(b_sz, t_len, D_H), s_new


def causal_depthwise_conv(x, hist, w, b):
    t_len = x.shape[1]
    xp = jnp.concatenate([hist.astype(x.dtype), x], axis=1)
    y = b + xp[:, 0:t_len] * w[0]
    for j in range(1, CONV_W):
        y = y + xp[:, j:j + t_len] * w[j]
    return y, xp[:, t_len:]


def block_diag_linear(x, w, b):
    b_sz, t_len, _ = x.shape
    xb = x.reshape(b_sz, t_len, RG_BLOCKS, RG_BLOCK_W)
    return jnp.einsum('btni,nij->btnj', xb, w).reshape(b_sz, t_len, D_R) + b


def lru_combine(left, right):
    a1, b1 = left
    a2, b2 = right
    return a1 * a2, a2 * b1 + b2


def rg_lru(xc, r_logit, i_logit, lam, h0, pos_offset):
    t_len = xc.shape[1]
    r = jax.nn.sigmoid(r_logit.astype(F32))
    gi = jax.nn.sigmoid(i_logit.astype(F32))
    log_a = -RG_C * r * jax.nn.softplus(-lam.astype(F32))
    a = jnp.exp(log_a)
    mult = jnp.sqrt(-jnp.expm1(2.0 * log_a))
    pos = pos_offset + jnp.arange(t_len)
    mult = jnp.where((pos == 0)[None, :, None], 1.0, mult)
    bterm = mult * gi * xc.astype(F32)
    bterm = bterm.at[:, 0].add(a[:, 0] * h0.astype(F32))
    _, h = lax.associative_scan(lru_combine, (a, bterm), axis=1)
    return h, h[:, -1]


def run_trunk(x, s_hgrn, conv_hist, h_rg, pos_offset, lb_all, pre_norm_g, w_in, hgrn_norm_g,
              w_branch_hgrn, conv_w, conv_b, w_gate_a, b_gate_a, w_gate_x, b_gate_x, rg_lambda,
              w_branch_rglru, w_out, post_norm_g):
    new_s, new_conv, new_h = [], [], []
    for layer in range(DEPTH):
        u = rms_norm(x, pre_norm_g[layer])
        proj = jnp.einsum('btd,dp->btp', u, w_in[layer])
        q_raw, f_raw, v_raw, g_h, x_r, g_r, m_h, m_r = jnp.split(proj, SPLITS, axis=-1)
        o_h, s_l = hgrn2_mixer(q_raw, f_raw, v_raw, s_hgrn[layer], lb_all[layer], layer, hgrn_norm_g[layer])
        y_h = (o_h * jax.nn.silu(g_h.astype(F32))).astype(x.dtype)
        xc, conv_l = causal_depthwise_conv(x_r, conv_hist[layer], conv_w[layer], conv_b[layer])
        r_logit = block_diag_linear(xc, w_gate_a[layer], b_gate_a[layer])
        i_logit = block_diag_linear(xc, w_gate_x[layer], b_gate_x[layer])
        h_seq, h_l = rg_lru(xc, r_logit, i_logit, rg_lambda[layer], h_rg[layer], pos_offset)
        y_r = (h_seq * jax.nn.silu(g_r.astype(F32))).astype(x.dtype)
        b_h = jnp.einsum('btc,cd->btd', y_h, w_branch_hgrn[layer])
        b_r = jnp.einsum('btc,cd->btd', y_r, w_branch_rglru[layer])
        merged = jax.nn.sigmoid(m_h) * b_h + jax.nn.sigmoid(m_r) * b_r
        out = jnp.einsum('btd,de->bte', merged, w_out[layer])
        x = x + rms_norm(out, post_norm_g[layer])
        new_s.append(s_l)
        new_conv.append(conv_l)
        new_h.append(h_l)
    return x, jnp.stack(new_s), jnp.stack(new_conv), jnp.stack(new_h)


def setup_inputs(seed: int = 0) -> dict:
    key = jax.random.key(seed)
    ks = jax.random.split(key, 24)
    nrm = jax.random.normal
    a0 = jax.random.uniform(ks[18], (DEPTH, D_R), minval=0.9, maxval=0.999)
    a_base = a0 ** (1.0 / RG_C)
    rg_lambda = jnp.log(a_base) - jnp.log1p(-a_base)
    return {
        'x_prompt': nrm(ks[0], (BATCH, SEQ, D_MODEL), F32),
        'x_sample': nrm(ks[1], (DEC_BATCH, DEC_SEQ, D_MODEL), F32),
        'state_hgrn': 0.5 * nrm(ks[2], (DEPTH, DEC_BATCH, HG_HEADS, HG_HEAD_K, HG_HEAD_V), F32),
        'state_conv': nrm(ks[3], (DEPTH, DEC_BATCH, CONV_W - 1, D_R), F32),
        'state_rglru': 0.5 * nrm(ks[4], (DEPTH, DEC_BATCH, D_R), F32),
        'lb_param': 0.1 * nrm(ks[5], (DEPTH, D_H), F32),
        'pre_norm_g': 1.0 + 0.02 * nrm(ks[6], (DEPTH, D_MODEL), F32),
        'w_in': nrm(ks[7], (DEPTH, D_MODEL, PROJ), F32) * D_MODEL ** -0.5,
        'hgrn_norm_g': 1.0 + 0.02 * nrm(ks[8], (DEPTH, D_H), F32),
        'w_branch_hgrn': nrm(ks[9], (DEPTH, D_H, D_MODEL), F32) * D_H ** -0.5,
        'conv_w': nrm(ks[10], (DEPTH, CONV_W, D_R), F32) * CONV_W ** -0.5,
        'conv_b': 0.01 * nrm(ks[11], (DEPTH, D_R), F32),
        'w_gate_a': nrm(ks[12], (DEPTH, RG_BLOCKS, RG_BLOCK_W, RG_BLOCK_W), F32) * RG_BLOCK_W ** -0.5,
        'b_gate_a': 0.01 * nrm(ks[13], (DEPTH, D_R), F32),
        'w_gate_x': nrm(ks[14], (DEPTH, RG_BLOCKS, RG_BLOCK_W, RG_BLOCK_W), F32) * RG_BLOCK_W ** -0.5,
        'b_gate_x': 0.01 * nrm(ks[15], (DEPTH, D_R), F32),
        'rg_lambda': rg_lambda,
        'w_branch_rglru': nrm(ks[16], (DEPTH, D_R, D_MODEL), F32) * D_R ** -0.5,
        'w_out': nrm(ks[17], (DEPTH, D_MODEL, D_MODEL), F32) * D_MODEL ** -0.5,
        'post_norm_g': 1.0 + 0.02 * nrm(ks[19], (DEPTH, D_MODEL), F32),
    }


def reference(x_prompt, x_sample, state_hgrn, state_conv, state_rglru, lb_param, pre_norm_g, w_in,
              hgrn_norm_g, w_branch_hgrn, conv_w, conv_b, w_gate_a, b_gate_a, w_gate_x, b_gate_x,
              rg_lambda, w_branch_rglru, w_out, post_norm_g):
    p = jax.nn.softmax(lb_param.astype(F32), axis=0)
    lb_all = jnp.cumsum(p, axis=0) - p[0]
    nb = x_prompt.shape[0]
    s0 = jnp.zeros((DEPTH, nb, HG_HEADS, HG_HEAD_K, HG_HEAD_V), F32)
    c0 = jnp.zeros((DEPTH, nb, CONV_W - 1, D_R), x_prompt.dtype)
    h0 = jnp.zeros((DEPTH, nb, D_R), F32)
    y_prompt, s_p, c_p, h_p = run_trunk(
        x_prompt, s0, c0, h0, 0, lb_all, pre_norm_g, w_in, hgrn_norm_g, w_branch_hgrn, conv_w, conv_b,
        w_gate_a, b_gate_a, w_gate_x, b_gate_x, rg_lambda, w_branch_rglru, w_out, post_norm_g)
    y_sample, s_s, c_s, h_s = run_trunk(
        x_sample, state_hgrn, state_conv, state_rglru, PAST_LEN, lb_all, pre_norm_g, w_in, hgrn_norm_g,
        w_branch_hgrn, conv_w, conv_b, w_gate_a, b_gate_a, w_gate_x, b_gate_x, rg_lambda, w_branch_rglru,
        w_out, post_norm_g)
    return (y_prompt, y_sample, s_p, c_p, h_p, s_s, c_s, h_s)
```

```python
import functools

import jax
import jax.numpy as jnp
from jax import lax
from jax.experimental import pallas as pl
from jax.experimental.pallas import tpu as pltpu

F32 = jnp.float32
BF16 = jnp.bfloat16

D_MODEL = 2048
HEADS = 16
HEAD_DIM = 128
RG_BLOCKS = 16
CONV_W = 4
RG_C = 8.0
EPS = 1e-6
LANES = 128
SUBLANES = 8
PROJ = 8 * D_MODEL
N_COL_BLOCKS = PROJ // LANES
CB_Q, CB_F, CB_V, CB_GH, CB_XR, CB_GR, CB_MH, CB_MR = (16 * i for i in range(8))
HGRN_CHUNK = 64
VMEM_LIMIT_BYTES = 56 * 1024 * 1024


def _sigmoid_pair(z):
    e = jnp.exp(-jnp.abs(z))
    inv = 1.0 / (1.0 + e)
    pos = z >= 0
    sig_pos = jnp.where(pos, 1.0, e) * inv
    sig_neg = jnp.where(pos, e, 1.0) * inv
    log_sig = jnp.minimum(z, 0.0) - jnp.log1p(e)
    return sig_pos, sig_neg, log_sig


def _silu(x):
    return x * jax.nn.sigmoid(x)


def _in_proj_kernel(x_ref, g_ref, w_ref, o_ref, u_ref, *, tn):
    @pl.when(pl.program_id(1) == 0)
    def _():
        x = x_ref[...]
        ms = jnp.mean(x * x, axis=-1, keepdims=True)
        u_ref[...] = (x * lax.rsqrt(ms + EPS) * g_ref[...]).astype(BF16)

    res = jnp.dot(u_ref[...], w_ref[...], preferred_element_type=F32)
    for c in range(tn // LANES):
        o_ref[c] = res[:, c * LANES:(c + 1) * LANES]


def _in_proj(x, g, w, *, tm, tn):
    m = x.shape[0]
    return pl.pallas_call(
        functools.partial(_in_proj_kernel, tn=tn),
        grid=(m // tm, PROJ // tn),
        in_specs=[
            pl.BlockSpec((tm, D_MODEL), lambda i, j: (i, 0)),
            pl.BlockSpec((1, D_MODEL), lambda i, j: (0, 0)),
            pl.BlockSpec((D_MODEL, tn), lambda i, j: (0, j)),
        ],
        out_specs=pl.BlockSpec((tn // LANES, tm, LANES), lambda i, j: (j, i, 0)),
        out_shape=jax.ShapeDtypeStruct((N_COL_BLOCKS, m, LANES), F32),
        scratch_shapes=[pltpu.VMEM((tm, D_MODEL), BF16)],
        compiler_params=pltpu.CompilerParams(
            dimension_semantics=("parallel", "arbitrary"), vmem_limit_bytes=VMEM_LIMIT_BYTES),
        name="in_proj",
    )(x, g, w)


def _hgrn_kernel(lbp_ref, ng_ref, q_ref, f_ref, v_ref, g_ref, s0_ref, y_ref, s_out_ref, st_ref,
                 *, layer, chunk, n_chunks, heads_per_step):
    tt = pl.program_id(2)
    n_groups = chunk // SUBLANES

    @pl.when(tt == 0)
    def _():
        for hh in range(heads_per_step):
            st_ref[hh] = s0_ref[0, hh].T

    sub = lax.broadcasted_iota(jnp.int32, (SUBLANES, LANES), 0)
    sub_c = lax.broadcasted_iota(jnp.int32, (SUBLANES, chunk), 0)
    col_c = lax.broadcasted_iota(jnp.int32, (SUBLANES, chunk), 1)

    def group_cumsum(x):
        for d in (1, 2, 4):
            x = x + jnp.where(sub >= d, pltpu.roll(x, d, 0), 0.0)
        return x

    def body(idx, carry):
        hh = idx // n_chunks
        ci = idx % n_chunks
        rows = pl.ds(pl.multiple_of(ci * chunk, chunk), chunk)
        q_raw = q_ref[hh, rows, :]
        z = f_ref[hh, rows, :]
        v = v_ref[hh, rows, :]
        g_h = g_ref[hh, rows, :]

        q = _silu(q_raw)
        sig_pos, sig_neg, log_sig = _sigmoid_pair(z)
        if layer == 0:
            log_f, k = log_sig, sig_neg
        else:
            lbp = lbp_ref[hh]
            ex = jnp.exp(lbp - jnp.max(lbp, axis=0, keepdims=True))
            p = ex / jnp.sum(ex, axis=0, keepdims=True)
            lb = jnp.sum(p[1:layer + 1], axis=0, keepdims=True)
            log_f = jnp.log(lb + (1.0 - lb) * sig_pos)
            k = (1.0 - lb) * sig_neg

        qs, ks, bs, ends = [], [], [], []
        run = None
        for g in range(n_groups):
            sl = slice(g * SUBLANES, (g + 1) * SUBLANES)
            bg = group_cumsum(log_f[sl])
            if run is not None:
                bg = bg + run
            run = jnp.broadcast_to(bg[SUBLANES - 1:SUBLANES, :], (SUBLANES, LANES))
            qs.append(q[sl]); ks.append(k[sl]); bs.append(bg); ends.append(run)
        b_end = ends[-1]

        p_rows = []
        for g in range(n_groups):
            acc = jnp.zeros((SUBLANES, chunk), F32)
            for d in range(SUBLANES):
                if d == 0:
                    a_d = qs[g] * ks[g]
                    valid = col_c == sub_c + g * SUBLANES
                else:
                    a_d = qs[g] * pltpu.roll(ks[g], d, 0) * jnp.exp(bs[g] - pltpu.roll(bs[g], d, 0))
                    valid = (col_c == sub_c + (g * SUBLANES - d)) & (sub_c >= d)
                acc = acc + jnp.where(valid, jnp.sum(a_d, axis=-1, keepdims=True), 0.0)
            p_rows.append(acc)

        half = chunk // 2
        while half >= SUBLANES:
            gph = half // SUBLANES
            q_parts, k_parts, q_groups = [], [], []
            for g in range(n_groups):
                first = (g // gph) * gph
                if (g // gph) % 2 == 1:
                    ref_row = ends[first - 1]
                    q_parts.append(qs[g] * jnp.exp(bs[g] - ref_row))
                    q_groups.append(g)
                    k_parts.append(jnp.zeros((SUBLANES, LANES), F32))
                else:
                    ref_row = ends[first + gph - 1]
                    k_parts.append(ks[g] * jnp.exp(ref_row - bs[g]))
            ql = jnp.concatenate(q_parts, axis=0).astype(BF16)
            kl = jnp.concatenate(k_parts, axis=0).astype(BF16)
            sc = lax.dot_general(ql, kl, (((1,), (1,)), ((), ())), preferred_element_type=F32)
            for j, g in enumerate(q_groups):
                lo = ((g // gph) - 1) * half
                valid = (col_c >= lo) & (col_c < lo + half)
                p_rows[g] = p_rows[g] + jnp.where(valid, sc[j * SUBLANES:(j + 1) * SUBLANES], 0.0)
            half //= 2

        p_mat = jnp.concatenate(p_rows, axis=0).astype(BF16)
        st = st_ref[hh]
        q_in = jnp.concatenate([qs[g] * jnp.exp(bs[g]) for g in range(n_groups)], axis=0)
        k_out = jnp.concatenate([ks[g] * jnp.exp(b_end - bs[g]) for g in range(n_groups)], axis=0)
        o = jnp.dot(p_mat, v.astype(BF16), preferred_element_type=F32)
        o = o + lax.dot_general(q_in.astype(BF16), st.astype(BF16), (((1,), (1,)), ((), ())),
                                preferred_element_type=F32)
        st_ref[hh] = st * jnp.exp(b_end[0:1, :]) + jnp.dot(
            v.T.astype(BF16), k_out.astype(BF16), preferred_element_type=F32)

        o = o * lax.rsqrt(jnp.mean(o * o, axis=-1, keepdims=True) + EPS)
        o = o * ng_ref[hh]
        y_ref[hh, rows, :] = (o * _silu(g_h)).astype(BF16)
        return carry

    lax.fori_loop(0, heads_per_step * n_chunks, body, 0)

    @pl.when(tt == pl.num_programs(2) - 1)
    def _():
        for hh in range(heads_per_step):
            s_out_ref[0, hh] = st_ref[hh].T


def _hgrn(proj, lbp, ng, s0, *, layer, batch, seq, tc, chunk, heads_per_step):
    m = proj.shape[1]
    nt = seq // tc
    hb = heads_per_step

    def sec(cb):
        return pl.BlockSpec((hb, tc, LANES), lambda b, h, t: (cb // hb + h, b * nt + t, 0))

    return pl.pallas_call(
        functools.partial(_hgrn_kernel, layer=layer, chunk=chunk, n_chunks=tc // chunk,
                          heads_per_step=hb),
        grid=(batch, HEADS // hb, nt),
        in_specs=[
            pl.BlockSpec((hb,) + lbp.shape[1:], lambda b, h, t: (h, 0, 0)),
            pl.BlockSpec((hb, 1, LANES), lambda b, h, t: (h, 0, 0)),
            sec(CB_Q), sec(CB_F), sec(CB_V), sec(CB_GH),
            pl.BlockSpec((1, hb, HEAD_DIM, HEAD_DIM), lambda b, h, t: (b, h, 0, 0)),
        ],
        out_specs=[
            pl.BlockSpec((hb, tc, LANES), lambda b, h, t: (h, b * nt + t, 0)),
            pl.BlockSpec((1, hb, HEAD_DIM, HEAD_DIM), lambda b, h, t: (b, h, 0, 0)),
        ],
        out_shape=[
            jax.ShapeDtypeStruct((HEADS, m, LANES), BF16),
            jax.ShapeDtypeStruct((batch, HEADS, HEAD_DIM, HEAD_DIM), F32),
        ],
        scratch_shapes=[pltpu.VMEM((hb, HEAD_DIM, HEAD_DIM), F32)],
        compiler_params=pltpu.CompilerParams(
            dimension_semantics=("parallel", "parallel", "arbitrary"),
            vmem_limit_bytes=VMEM_LIMIT_BYTES),
        name="hgrn2",
    )(lbp, ng, proj, proj, proj, proj, s0)


def _rglru_kernel(xr_ref, gr_ref, hist_ref, h0_ref, cw_ref, cb_ref, wa_ref, ba_ref, wx_ref, bx_ref,
                  lam_ref, y_ref, conv_out_ref, h_out_ref, xbuf_ref, hcar_ref, a_ref, b_ref, h_ref,
                  *, tc, blocks_per_step, stream_start):
    tt = pl.program_id(2)
    last = tt == pl.num_programs(2) - 1
    hist_lo = SUBLANES - (CONV_W - 1)
    n_groups = tc // SUBLANES
    sub = lax.broadcasted_iota(jnp.int32, (SUBLANES, LANES), 0)
    row = lax.broadcasted_iota(jnp.int32, (tc, LANES), 0)

    @pl.when(tt == 0)
    def _():
        for jj in range(blocks_per_step):
            xbuf_ref[jj, hist_lo:SUBLANES, :] = hist_ref[0, jj]
            hcar_ref[jj] = jnp.broadcast_to(h0_ref[0, jj], (SUBLANES, LANES))

    def block_body(jj, carry):
        x = xr_ref[jj]
        xbuf_ref[jj, SUBLANES:SUBLANES + tc, :] = x
        cw = cw_ref[jj]
        xc = cb_ref[jj] + xbuf_ref[jj, pl.ds(hist_lo, tc), :] * cw[0:1, :]
        for j in range(1, CONV_W):
            xc = xc + xbuf_ref[jj, pl.ds(hist_lo + j, tc), :] * cw[j:j + 1, :]
        tail = x[tc - (CONV_W - 1):tc, :]
        xbuf_ref[jj, hist_lo:SUBLANES, :] = tail

        xcb = xc.astype(BF16)
        r = jax.nn.sigmoid(jnp.dot(xcb, wa_ref[jj], preferred_element_type=F32) + ba_ref[jj])
        gi = jax.nn.sigmoid(jnp.dot(xcb, wx_ref[jj], preferred_element_type=F32) + bx_ref[jj])
        nl = -lam_ref[jj]
        softplus_nl = jnp.maximum(nl, 0.0) + jnp.log1p(jnp.exp(-jnp.abs(nl)))
        log_a = -RG_C * r * softplus_nl
        a = jnp.exp(log_a)
        mult = jnp.sqrt(-jnp.tanh(log_a) * (a * a + 1.0))
        if stream_start:
            mult = jnp.where((row == 0) & (tt == 0), 1.0, mult)
        a_ref[...] = a
        b_ref[...] = mult * gi * xc

        def scan_body(g, h_prev):
            rows = pl.ds(pl.multiple_of(g * SUBLANES, SUBLANES), SUBLANES)
            ag = a_ref[rows, :]
            bg = b_ref[rows, :]
            for d in (1, 2, 4):
                keep = sub >= d
                bg = ag * jnp.where(keep, pltpu.roll(bg, d, 0), 0.0) + bg
                ag = ag * jnp.where(keep, pltpu.roll(ag, d, 0), 1.0)
            hg = ag * h_prev + bg
            h_ref[rows, :] = hg
            return jnp.broadcast_to(hg[SUBLANES - 1:SUBLANES, :], (SUBLANES, LANES))

        h_last = lax.fori_loop(0, n_groups, scan_body, hcar_ref[jj], unroll=min(8, n_groups))
        hcar_ref[jj] = h_last
        y_ref[jj] = (h_ref[...] * _silu(gr_ref[jj])).astype(BF16)

        @pl.when(last)
        def _():
            conv_out_ref[0, jj] = tail
            h_out_ref[0, jj] = h_last[0:1, :]
        return carry

    lax.fori_loop(0, blocks_per_step, block_body, 0)


def _rglru(proj, hist, h0, cw, cb, wa, ba, wx, bx, lam, *, batch, seq, tc, blocks_per_step,
           stream_start):
    m = proj.shape[1]
    nt = seq // tc
    jb = blocks_per_step

    def sec(cb_off):
        return pl.BlockSpec((jb, tc, LANES), lambda b, j, t: (cb_off // jb + j, b * nt + t, 0))

    def per_block(shape):
        return pl.BlockSpec((jb,) + shape, lambda b, j, t: (j,) + (0,) * len(shape))

    return pl.pallas_call(
        functools.partial(_rglru_kernel, tc=tc, blocks_per_step=jb, stream_start=stream_start),
        grid=(batch, RG_BLOCKS // jb, nt),
        in_specs=[
            sec(CB_XR), sec(CB_GR),
            pl.BlockSpec((1, jb, CONV_W - 1, LANES), lambda b, j, t: (b, j, 0, 0)),
            pl.BlockSpec((1, jb, 1, LANES), lambda b, j, t: (b, j, 0, 0)),
            per_block((CONV_W, LANES)), per_block((1, LANES)),
            per_block((LANES, LANES)), per_block((1, LANES)),
            per_block((LANES, LANES)), per_block((1, LANES)),
            per_block((1, LANES)),
        ],
        out_specs=[
            pl.BlockSpec((jb, tc, LANES), lambda b, j, t: (j, b * nt + t, 0)),
            pl.BlockSpec((1, jb, CONV_W - 1, LANES), lambda b, j, t: (b, j, 0, 0)),
            pl.BlockSpec((1, jb, 1, LANES), lambda b, j, t: (b, j, 0, 0)),
        ],
        out_shape=[
            jax.ShapeDtypeStruct((RG_BLOCKS, m, LANES), BF16),
            jax.ShapeDtypeStruct((batch, RG_BLOCKS, CONV_W - 1, LANES), F32),
            jax.ShapeDtypeStruct((batch, RG_BLOCKS, 1, LANES), F32),
        ],
        scratch_shapes=[
            pltpu.VMEM((jb, SUBLANES + tc, LANES), F32),
            pltpu.VMEM((jb, SUBLANES, LANES), F32),
            pltpu.VMEM((tc, LANES), F32),
            pltpu.VMEM((tc, LANES), F32),
            pltpu.VMEM((tc, LANES), F32),
        ],
        compiler_params=pltpu.CompilerParams(
            dimension_semantics=("parallel", "parallel", "arbitrary"),
            vmem_limit_bytes=VMEM_LIMIT_BYTES),
        name="rglru",
    )(proj, proj, hist, h0, cw, cb, wa, ba, wx, bx, lam)


def _out_kernel(yh_ref, yr_ref, mh_ref, mr_ref, x_ref, wbh_ref, wbr_ref, wo_ref, g_ref, o_ref):
    def gather(ref):
        return jnp.concatenate([ref[c] for c in range(D_MODEL // LANES)], axis=1)

    b_h = jnp.dot(gather(yh_ref), wbh_ref[...], preferred_element_type=F32)
    b_r = jnp.dot(gather(yr_ref), wbr_ref[...], preferred_element_type=F32)
    merged = jax.nn.sigmoid(gather(mh_ref)) * b_h + jax.nn.sigmoid(gather(mr_ref)) * b_r
    out = jnp.dot(merged.astype(BF16), wo_ref[...], preferred_element_type=F32)
    ms = jnp.mean(out * out, axis=-1, keepdims=True)
    o_ref[...] = x_ref[...] + out * lax.rsqrt(ms + EPS) * g_ref[...]


def _out_stage(yh, yr, proj, x, wbh, wbr, wo, g, *, tm):
    m = x.shape[0]
    nb = D_MODEL // LANES

    def resident(shape):
        return pl.BlockSpec(shape, lambda i: (0,) * len(shape), pipeline_mode=pl.Buffered(1))

    return pl.pallas_call(
        _out_kernel,
        grid=(m // tm,),
        in_specs=[
            pl.BlockSpec((nb, tm, LANES), lambda i: (0, i, 0)),
            pl.BlockSpec((nb, tm, LANES), lambda i: (0, i, 0)),
            pl.BlockSpec((nb, tm, LANES), lambda i: (CB_MH // nb, i, 0)),
            pl.BlockSpec((nb, tm, LANES), lambda i: (CB_MR // nb, i, 0)),
            pl.BlockSpec((tm, D_MODEL), lambda i: (i, 0)),
            resident((D_MODEL, D_MODEL)), resident((D_MODEL, D_MODEL)), resident((D_MODEL, D_MODEL)),
            resident((1, D_MODEL)),
        ],
        out_specs=pl.BlockSpec((tm, D_MODEL), lambda i: (i, 0)),
        out_shape=jax.ShapeDtypeStruct((m, D_MODEL), F32),
        compiler_params=pltpu.CompilerParams(
            dimension_semantics=("parallel",), vmem_limit_bytes=VMEM_LIMIT_BYTES),
        name="out_stage",
    )(yh, yr, proj, proj, x, wbh, wbr, wo, g)


def _stream_tiles(batch, seq):
    m = batch * seq
    tm_in = min(m, 1024)
    tm_out = min(m, 256)
    tc = min(seq, 1024)
    chunk = min(seq, HGRN_CHUNK)
    per_step = 1 if seq >= 1024 else HEADS
    return dict(tm_in=tm_in, tm_out=tm_out, tc=tc, chunk=chunk, per_step=per_step)


def _run_trunk(x, s_hgrn, conv_hist, h_rg, stream_start, w):
    batch, seq, _ = x.shape
    m = batch * seq
    t = _stream_tiles(batch, seq)
    xf = x.reshape(m, D_MODEL)
    new_s, new_conv, new_h = [], [], []
    for layer in range(len(w["w_in"])):
        proj = _in_proj(xf, w["pre_norm_g"][layer], w["w_in"][layer], tm=t["tm_in"], tn=1024)
        yh, s_l = _hgrn(proj, w["lbp"], w["hgrn_norm_g"][layer], s_hgrn[layer], layer=layer,
                        batch=batch, seq=seq, tc=t["tc"], chunk=t["chunk"],
                        heads_per_step=t["per_step"])
        hist = conv_hist[layer].reshape(batch, CONV_W - 1, RG_BLOCKS, LANES).transpose(0, 2, 1, 3)
        h0 = h_rg[layer].reshape(batch, RG_BLOCKS, 1, LANES)
        yr, conv_l, h_l = _rglru(
            proj, hist, h0, w["conv_w"][layer], w["conv_b"][layer], w["w_gate_a"][layer],
            w["b_gate_a"][layer], w["w_gate_x"][layer], w["b_gate_x"][layer], w["rg_lambda"][layer],
            batch=batch, seq=seq, tc=t["tc"], blocks_per_step=t["per_step"],
            stream_start=stream_start)
        xf = _out_stage(yh, yr, proj, xf, w["w_branch_hgrn"][layer], w["w_branch_rglru"][layer],
                        w["w_out"][layer], w["post_norm_g"][layer], tm=t["tm_out"])
        new_s.append(s_l)
        new_conv.append(conv_l.transpose(0, 2, 1, 3).reshape(batch, CONV_W - 1, D_MODEL))
        new_h.append(h_l.reshape(batch, D_MODEL))
    return xf.reshape(batch, seq, D_MODEL), jnp.stack(new_s), jnp.stack(new_conv), jnp.stack(new_h)


def _prep_weights(lb_param, pre_norm_g, w_in, hgrn_norm_g, w_branch_hgrn, conv_w, conv_b, w_gate_a,
                  b_gate_a, w_gate_x, b_gate_x, rg_lambda, w_branch_rglru, w_out, post_norm_g):
    depth = w_in.shape[0]

    def rows(a):
        return a.reshape(depth, 1, D_MODEL)

    def blocks(a):
        return a.reshape(depth, RG_BLOCKS, 1, LANES)

    return dict(
        lbp=lb_param.astype(F32).reshape(depth, HEADS, LANES).transpose(1, 0, 2),
        pre_norm_g=rows(pre_norm_g), post_norm_g=rows(post_norm_g),
        hgrn_norm_g=hgrn_norm_g.reshape(depth, HEADS, 1, LANES),
        w_in=w_in.astype(BF16), w_branch_hgrn=w_branch_hgrn.astype(BF16),
        w_branch_rglru=w_branch_rglru.astype(BF16), w_out=w_out.astype(BF16),
        conv_w=conv_w.reshape(depth, CONV_W, RG_BLOCKS, LANES).transpose(0, 2, 1, 3),
        conv_b=blocks(conv_b), w_gate_a=w_gate_a.astype(BF16), b_gate_a=blocks(b_gate_a),
        w_gate_x=w_gate_x.astype(BF16), b_gate_x=blocks(b_gate_x), rg_lambda=blocks(rg_lambda),
    )


def kernel(x_prompt, x_sample, state_hgrn, state_conv, state_rglru, lb_param, pre_norm_g, w_in,
           hgrn_norm_g, w_branch_hgrn, conv_w, conv_b, w_gate_a, b_gate_a, w_gate_x, b_gate_x,
           rg_lambda, w_branch_rglru, w_out, post_norm_g):
    w = _prep_weights(lb_param, pre_norm_g, w_in, hgrn_norm_g, w_branch_hgrn, conv_w, conv_b,
                      w_gate_a, b_gate_a, w_gate_x, b_gate_x, rg_lambda, w_branch_rglru, w_out,
                      post_norm_g)
    depth = w_in.shape[0]
    nb = x_prompt.shape[0]
    s0 = jnp.zeros((depth, nb, HEADS, HEAD_DIM, HEAD_DIM), F32)
    c0 = jnp.zeros((depth, nb, CONV_W - 1, D_MODEL), F32)
    h0 = jnp.zeros((depth, nb, D_MODEL), F32)
    y_p, s_p, c_p, h_p = _run_trunk(x_prompt, s0, c0, h0, True, w)
    y_s, s_s, c_s, h_s = _run_trunk(x_sample, state_hgrn, state_conv, state_rglru, False, w)
    return (y_p, y_s, s_p, c_p, h_p, s_s, c_s, h_s)
```

```python
import functools

import jax
import jax.numpy as jnp
from jax import lax
from jax.experimental import pallas as pl
from jax.experimental.pallas import tpu as pltpu

F32 = jnp.float32
BF16 = jnp.bfloat16

D_MODEL = 2048
HEADS = 16
HEAD_DIM = 128
RG_BLOCKS = 16
CONV_W = 4
RG_C = 8.0
EPS = 1e-6
LANES = 128
SUBLANES = 8
PROJ = 8 * D_MODEL
N_COL_BLOCKS = PROJ // LANES
CB_Q, CB_F, CB_V, CB_GH, CB_XR, CB_GR, CB_MH, CB_MR = (16 * i for i in range(8))
HGRN_CHUNK = 64
HGRN_UNROLL = 8
LOG2E = 1.4426950408889634
VMEM_LIMIT_BYTES = 56 * 1024 * 1024


def _sigmoid_pair(z):
    e = jnp.exp(-jnp.abs(z))
    inv = 1.0 / (1.0 + e)
    pos = z >= 0
    sig_pos = jnp.where(pos, 1.0, e) * inv
    sig_neg = jnp.where(pos, e, 1.0) * inv
    log_sig = jnp.minimum(z, 0.0) - jnp.log1p(e)
    return sig_pos, sig_neg, log_sig


def _silu(x):
    return x * jax.nn.sigmoid(x)


def _in_proj_kernel(x_ref, g_ref, w_ref, o_ref, u_ref, *, tn):
    @pl.when(pl.program_id(1) == 0)
    def _():
        x = x_ref[...]
        ms = jnp.mean(x * x, axis=-1, keepdims=True)
        u_ref[...] = (x * lax.rsqrt(ms + EPS) * g_ref[...]).astype(BF16)

    res = jnp.dot(u_ref[...], w_ref[...], preferred_element_type=F32)
    for c in range(tn // LANES):
        o_ref[c] = res[:, c * LANES:(c + 1) * LANES]


def _in_proj(x, g, w, *, tm, tn):
    m = x.shape[0]
    return pl.pallas_call(
        functools.partial(_in_proj_kernel, tn=tn),
        grid=(m // tm, PROJ // tn),
        in_specs=[
            pl.BlockSpec((tm, D_MODEL), lambda i, j: (i, 0)),
            pl.BlockSpec((1, D_MODEL), lambda i, j: (0, 0)),
            pl.BlockSpec((D_MODEL, tn), lambda i, j: (0, j)),
        ],
        out_specs=pl.BlockSpec((tn // LANES, tm, LANES), lambda i, j: (j, i, 0)),
        out_shape=jax.ShapeDtypeStruct((N_COL_BLOCKS, m, LANES), F32),
        scratch_shapes=[pltpu.VMEM((tm, D_MODEL), BF16)],
        compiler_params=pltpu.CompilerParams(
            dimension_semantics=("parallel", "arbitrary"), vmem_limit_bytes=VMEM_LIMIT_BYTES),
        name="in_proj",
    )(x, g, w)


def _hgrn_chunk(q_raw, z, v, g_h, st, lb, ng, kb_ref, consts, *, chunk):
    shift_ok, diag_c, col_c = consts
    n_groups = chunk // SUBLANES

    q = _silu(q_raw)
    sig_pos, sig_neg, log_sig = _sigmoid_pair(z)
    if lb is None:
        log_f, k = log_sig, sig_neg
    else:
        log_f = jnp.log(lb + (1.0 - lb) * sig_pos)
        k = (1.0 - lb) * sig_neg
    lf2 = log_f * LOG2E

    qs, ks, bs, ends = [], [], [], []
    run = None
    for g in range(n_groups):
        sl = slice(g * SUBLANES, (g + 1) * SUBLANES)
        lf = lf2[sl]
        bg = lf
        for ok, d in zip(shift_ok, (1, 2, 4)):
            bg = bg + jnp.where(ok, pltpu.roll(bg, d, 0), 0.0)
        tot = jnp.sum(lf, axis=0, keepdims=True)
        if run is not None:
            bg = bg + run
            tot = tot + run
        run = tot
        qs.append(q[sl]); ks.append(k[sl]); bs.append(bg); ends.append(run)
    b_end = ends[-1]
    kb_ref[0, SUBLANES:SUBLANES + chunk, :] = k
    kb_ref[1, SUBLANES:SUBLANES + chunk, :] = jnp.concatenate(bs, axis=0)

    p_rows = []
    for g in range(n_groups):
        acc = jnp.zeros((SUBLANES, chunk), F32)
        for d in range(SUBLANES):
            if d == 0:
                a_d = qs[g] * ks[g]
            else:
                lo = SUBLANES + g * SUBLANES - d
                a_d = qs[g] * kb_ref[0, lo:lo + SUBLANES, :] * jnp.exp2(
                    bs[g] - kb_ref[1, lo:lo + SUBLANES, :])
            acc = jnp.where(diag_c == g * SUBLANES - d, jnp.sum(a_d, axis=-1, keepdims=True), acc)
        p_rows.append(jnp.where(col_c >= g * SUBLANES, acc, 0.0) if g else acc)

    half = chunk // 2
    while half >= SUBLANES:
        gph = half // SUBLANES
        q_parts, k_parts, q_groups = [], [], []
        for g in range(n_groups):
            first = (g // gph) * gph
            if (g // gph) % 2 == 1:
                q_parts.append(qs[g] * jnp.exp2(bs[g] - ends[first - 1]))
                q_groups.append(g)
                k_parts.append(jnp.zeros((SUBLANES, LANES), F32))
            else:
                k_parts.append(ks[g] * jnp.exp2(ends[first + gph - 1] - bs[g]))
        ql = jnp.concatenate(q_parts, axis=0).astype(BF16)
        kl = jnp.concatenate(k_parts, axis=0).astype(BF16)
        sc = lax.dot_general(ql, kl, (((1,), (1,)), ((), ())), preferred_element_type=F32)
        for j, g in enumerate(q_groups):
            lo = ((g // gph) - 1) * half
            valid = (col_c >= lo) & (col_c < lo + half)
            p_rows[g] = p_rows[g] + jnp.where(valid, sc[j * SUBLANES:(j + 1) * SUBLANES], 0.0)
        half //= 2

    p_mat = jnp.concatenate(p_rows, axis=0).astype(BF16)
    q_in = jnp.concatenate([qs[g] * jnp.exp2(bs[g]) for g in range(n_groups)], axis=0)
    k_out = jnp.concatenate([ks[g] * jnp.exp2(b_end - bs[g]) for g in range(n_groups)], axis=0)
    o = jnp.dot(p_mat, v.astype(BF16), preferred_element_type=F32)
    o = o + lax.dot_general(q_in.astype(BF16), st.astype(BF16), (((1,), (1,)), ((), ())),
                            preferred_element_type=F32)
    st_new = st * jnp.exp2(b_end) + jnp.dot(v.T.astype(BF16), k_out.astype(BF16),
                                            preferred_element_type=F32)
    o = o * lax.rsqrt(jnp.mean(o * o, axis=-1, keepdims=True) + EPS)
    return o * ng * _silu(g_h), st_new


def _hgrn_consts(chunk):
    sub = lax.broadcasted_iota(jnp.int32, (SUBLANES, LANES), 0)
    sub_c = lax.broadcasted_iota(jnp.int32, (SUBLANES, chunk), 0)
    col_c = lax.broadcasted_iota(jnp.int32, (SUBLANES, chunk), 1)
    return [sub >= d for d in (1, 2, 4)], col_c - sub_c, col_c


def _lower_bound(lbp, layer):
    if layer == 0:
        return None
    ex = jnp.exp(lbp - jnp.max(lbp, axis=0, keepdims=True))
    p = ex / jnp.sum(ex, axis=0, keepdims=True)
    return jnp.sum(p[1:layer + 1], axis=0, keepdims=True)


def _hgrn_kernel(lbp_ref, ng_ref, q_ref, f_ref, v_ref, g_ref, s0_ref, y_ref, s_out_ref, st_ref, kb_ref,
                 *, layer, chunk, n_chunks, heads_per_step, unroll):
    tt = pl.program_id(2)

    @pl.when(tt == 0)
    def _():
        for hh in range(heads_per_step):
            st_ref[hh] = s0_ref[0, hh].T

    for u in range(unroll):
        kb_ref[u, :, 0:SUBLANES, :] = jnp.zeros((2, SUBLANES, LANES), F32)
    consts = _hgrn_consts(chunk)
    n_iter = n_chunks // unroll

    def body(idx, carry):
        hh = idx // n_iter
        base = (idx % n_iter) * (unroll * chunk)
        lb = _lower_bound(lbp_ref[hh], layer)
        st = st_ref[hh]
        for u in range(unroll):
            rows = pl.ds(pl.multiple_of(base + u * chunk, chunk), chunk)
            y, st = _hgrn_chunk(q_ref[hh, rows, :], f_ref[hh, rows, :], v_ref[hh, rows, :],
                                g_ref[hh, rows, :], st, lb, ng_ref[hh], kb_ref.at[u], consts,
                                chunk=chunk)
            y_ref[hh, rows, :] = y.astype(BF16)
        st_ref[hh] = st
        return carry

    lax.fori_loop(0, heads_per_step * n_iter, body, 0)

    @pl.when(tt == pl.num_programs(2) - 1)
    def _():
        for hh in range(heads_per_step):
            s_out_ref[0, hh] = st_ref[hh].T


def _hgrn(proj, lbp, ng, s0, *, layer, batch, seq, tc, chunk, heads_per_step):
    m = proj.shape[1]
    nt = seq // tc
    hb = heads_per_step
    unroll = min(HGRN_UNROLL, tc // chunk)

    def sec(cb):
        return pl.BlockSpec((hb, tc, LANES), lambda b, h, t: (cb // hb + h, b * nt + t, 0))

    return pl.pallas_call(
        functools.partial(_hgrn_kernel, layer=layer, chunk=chunk, n_chunks=tc // chunk,
                          heads_per_step=hb, unroll=unroll),
        grid=(batch, HEADS // hb, nt),
        in_specs=[
            pl.BlockSpec((hb,) + lbp.shape[1:], lambda b, h, t: (h, 0, 0)),
            pl.BlockSpec((hb, 1, LANES), lambda b, h, t: (h, 0, 0)),
            sec(CB_Q), sec(CB_F), sec(CB_V), sec(CB_GH),
            pl.BlockSpec((1, hb, HEAD_DIM, HEAD_DIM), lambda b, h, t: (b, h, 0, 0)),
        ],
        out_specs=[
            pl.BlockSpec((hb, tc, LANES), lambda b, h, t: (h, b * nt + t, 0)),
            pl.BlockSpec((1, hb, HEAD_DIM, HEAD_DIM), lambda b, h, t: (b, h, 0, 0)),
        ],
        out_shape=[
            jax.ShapeDtypeStruct((HEADS, m, LANES), BF16),
            jax.ShapeDtypeStruct((batch, HEADS, HEAD_DIM, HEAD_DIM), F32),
        ],
        scratch_shapes=[pltpu.VMEM((hb, HEAD_DIM, HEAD_DIM), F32),
                        pltpu.VMEM((unroll, 2, SUBLANES + chunk, LANES), F32)],
        compiler_params=pltpu.CompilerParams(
            dimension_semantics=("parallel", "parallel", "arbitrary"),
            vmem_limit_bytes=VMEM_LIMIT_BYTES),
        name="hgrn2",
    )(lbp, ng, proj, proj, proj, proj, s0)


def _rglru_kernel(xr_ref, gr_ref, hist_ref, h0_ref, cw_ref, cb_ref, wa_ref, ba_ref, wx_ref, bx_ref,
                  lam_ref, y_ref, conv_out_ref, h_out_ref, xbuf_ref, hcar_ref, a_ref, b_ref, h_ref,
                  *, tc, blocks_per_step, stream_start):
    tt = pl.program_id(2)
    last = tt == pl.num_programs(2) - 1
    hist_lo = SUBLANES - (CONV_W - 1)
    n_groups = tc // SUBLANES
    sub = lax.broadcasted_iota(jnp.int32, (SUBLANES, LANES), 0)
    row = lax.broadcasted_iota(jnp.int32, (tc, LANES), 0)

    @pl.when(tt == 0)
    def _():
        for jj in range(blocks_per_step):
            xbuf_ref[jj, hist_lo:SUBLANES, :] = hist_ref[0, jj]
            hcar_ref[jj] = jnp.broadcast_to(h0_ref[0, jj], (SUBLANES, LANES))

    def block_body(jj, carry):
        x = xr_ref[jj]
        xbuf_ref[jj, SUBLANES:SUBLANES + tc, :] = x
        cw = cw_ref[jj]
        xc = cb_ref[jj] + xbuf_ref[jj, pl.ds(hist_lo, tc), :] * cw[0:1, :]
        for j in range(1, CONV_W):
            xc = xc + xbuf_ref[jj, pl.ds(hist_lo + j, tc), :] * cw[j:j + 1, :]
        tail = x[tc - (CONV_W - 1):tc, :]
        xbuf_ref[jj, hist_lo:SUBLANES, :] = tail

        xcb = xc.astype(BF16)
        r = jax.nn.sigmoid(jnp.dot(xcb, wa_ref[jj], preferred_element_type=F32) + ba_ref[jj])
        gi = jax.nn.sigmoid(jnp.dot(xcb, wx_ref[jj], preferred_element_type=F32) + bx_ref[jj])
        nl = -lam_ref[jj]
        softplus_nl = jnp.maximum(nl, 0.0) + jnp.log1p(jnp.exp(-jnp.abs(nl)))
        log_a = -RG_C * r * softplus_nl
        a = jnp.exp(log_a)
        m2 = -jnp.tanh(log_a) * (a * a + 1.0)
        mult = jnp.where(m2 > 0.0, m2 * lax.rsqrt(m2), 0.0)
        if stream_start:
            mult = jnp.where((row == 0) & (tt == 0), 1.0, mult)
        a_ref[...] = a
        b_ref[...] = mult * gi * xc

        def scan_body(g, h_prev):
            rows = pl.ds(pl.multiple_of(g * SUBLANES, SUBLANES), SUBLANES)
            ag = a_ref[rows, :]
            bg = b_ref[rows, :]
            for d in (1, 2, 4):
                keep = sub >= d
                bg = ag * jnp.where(keep, pltpu.roll(bg, d, 0), 0.0) + bg
                ag = ag * jnp.where(keep, pltpu.roll(ag, d, 0), 1.0)
            hg = ag * h_prev + bg
            h_ref[rows, :] = hg
            return jnp.broadcast_to(hg[SUBLANES - 1:SUBLANES, :], (SUBLANES, LANES))

        h_last = lax.fori_loop(0, n_groups, scan_body, hcar_ref[jj], unroll=min(8, n_groups))
        hcar_ref[jj] = h_last
        y_ref[jj] = (h_ref[...] * _silu(gr_ref[jj])).astype(BF16)

        @pl.when(last)
        def _():
            conv_out_ref[0, jj] = tail
            h_out_ref[0, jj] = h_last[0:1, :]
        return carry

    lax.fori_loop(0, blocks_per_step, block_body, 0)


def _rglru(proj, hist, h0, cw, cb, wa, ba, wx, bx, lam, *, batch, seq, tc, blocks_per_step,
           stream_start):
    m = proj.shape[1]
    nt = seq // tc
    jb = blocks_per_step

    def sec(cb_off):
        return pl.BlockSpec((jb, tc, LANES), lambda b, j, t: (cb_off // jb + j, b * nt + t, 0))

    def per_block(shape):
        return pl.BlockSpec((jb,) + shape, lambda b, j, t: (j,) + (0,) * len(shape))

    return pl.pallas_call(
        functools.partial(_rglru_kernel, tc=tc, blocks_per_step=jb, stream_start=stream_start),
        grid=(batch, RG_BLOCKS // jb, nt),
        in_specs=[
            sec(CB_XR), sec(CB_GR),
            pl.BlockSpec((1, jb, CONV_W - 1, LANES), lambda b, j, t: (b, j, 0, 0)),
            pl.BlockSpec((1, jb, 1, LANES), lambda b, j, t: (b, j, 0, 0)),
            per_block((CONV_W, LANES)), per_block((1, LANES)),
            per_block((LANES, LANES)), per_block((1, LANES)),
            per_block((LANES, LANES)), per_block((1, LANES)),
            per_block((1, LANES)),
        ],
        out_specs=[
            pl.BlockSpec((jb, tc, LANES), lambda b, j, t: (j, b * nt + t, 0)),
            pl.BlockSpec((1, jb, CONV_W - 1, LANES), lambda b, j, t: (b, j, 0, 0)),
            pl.BlockSpec((1, jb, 1, LANES), lambda b, j, t: (b, j, 0, 0)),
        ],
        out_shape=[
            jax.ShapeDtypeStruct((RG_BLOCKS, m, LANES), BF16),
            jax.ShapeDtypeStruct((batch, RG_BLOCKS, CONV_W - 1, LANES), F32),
            jax.ShapeDtypeStruct((batch, RG_BLOCKS, 1, LANES), F32),
        ],
        scratch_shapes=[
            pltpu.VMEM((jb, SUBLANES + tc, LANES), F32),
            pltpu.VMEM((jb, SUBLANES, LANES), F32),
            pltpu.VMEM((tc, LANES), F32),
            pltpu.VMEM((tc, LANES), F32),
            pltpu.VMEM((tc, LANES), F32),
        ],
        compiler_params=pltpu.CompilerParams(
            dimension_semantics=("parallel", "parallel", "arbitrary"),
            vmem_limit_bytes=VMEM_LIMIT_BYTES),
        name="rglru",
    )(proj, proj, hist, h0, cw, cb, wa, ba, wx, bx, lam)


def _out_kernel(yh_ref, yr_ref, mh_ref, mr_ref, x_ref, wbh_ref, wbr_ref, wo_ref, g_ref, o_ref):
    def gather(ref):
        return jnp.concatenate([ref[c] for c in range(D_MODEL // LANES)], axis=1)

    b_h = jnp.dot(gather(yh_ref), wbh_ref[...], preferred_element_type=F32)
    b_r = jnp.dot(gather(yr_ref), wbr_ref[...], preferred_element_type=F32)
    merged = jax.nn.sigmoid(gather(mh_ref)) * b_h + jax.nn.sigmoid(gather(mr_ref)) * b_r
    out = jnp.dot(merged.astype(BF16), wo_ref[...], preferred_element_type=F32)
    ms = jnp.mean(out * out, axis=-1, keepdims=True)
    o_ref[...] = x_ref[...] + out * lax.rsqrt(ms + EPS) * g_ref[...]


def _out_stage(yh, yr, proj, x, wbh, wbr, wo, g, *, tm):
    m = x.shape[0]
    nb = D_MODEL // LANES

    def resident(shape):
        return pl.BlockSpec(shape, lambda i: (0,) * len(shape), pipeline_mode=pl.Buffered(1))

    return pl.pallas_call(
        _out_kernel,
        grid=(m // tm,),
        in_specs=[
            pl.BlockSpec((nb, tm, LANES), lambda i: (0, i, 0)),
            pl.BlockSpec((nb, tm, LANES), lambda i: (0, i, 0)),
            pl.BlockSpec((nb, tm, LANES), lambda i: (CB_MH // nb, i, 0)),
            pl.BlockSpec((nb, tm, LANES), lambda i: (CB_MR // nb, i, 0)),
            pl.BlockSpec((tm, D_MODEL), lambda i: (i, 0)),
            resident((D_MODEL, D_MODEL)), resident((D_MODEL, D_MODEL)), resident((D_MODEL, D_MODEL)),
            resident((1, D_MODEL)),
        ],
        out_specs=pl.BlockSpec((tm, D_MODEL), lambda i: (i, 0)),
        out_shape=jax.ShapeDtypeStruct((m, D_MODEL), F32),
        compiler_params=pltpu.CompilerParams(
            dimension_semantics=("parallel",), vmem_limit_bytes=VMEM_LIMIT_BYTES),
        name="out_stage",
    )(yh, yr, proj, proj, x, wbh, wbr, wo, g)


def _stream_tiles(batch, seq):
    m = batch * seq
    tm_in = min(m, 1024)
    tm_out = min(m, 256)
    tc = min(seq, 1024)
    chunk = min(seq, HGRN_CHUNK)
    per_step = 1 if seq >= 1024 else HEADS
    return dict(tm_in=tm_in, tm_out=tm_out, tc=tc, chunk=chunk, per_step=per_step)


def _run_trunk(x, s_hgrn, conv_hist, h_rg, stream_start, w):
    batch, seq, _ = x.shape
    m = batch * seq
    t = _stream_tiles(batch, seq)
    xf = x.reshape(m, D_MODEL)
    new_s, new_conv, new_h = [], [], []
    for layer in range(len(w["w_in"])):
        proj = _in_proj(xf, w["pre_norm_g"][layer], w["w_in"][layer], tm=t["tm_in"], tn=1024)
        yh, s_l = _hgrn(proj, w["lbp"], w["hgrn_norm_g"][layer], s_hgrn[layer], layer=layer,
                        batch=batch, seq=seq, tc=t["tc"], chunk=t["chunk"],
                        heads_per_step=t["per_step"])
        hist = conv_hist[layer].reshape(batch, CONV_W - 1, RG_BLOCKS, LANES).transpose(0, 2, 1, 3)
        h0 = h_rg[layer].reshape(batch, RG_BLOCKS, 1, LANES)
        yr, conv_l, h_l = _rglru(
            proj, hist, h0, w["conv_w"][layer], w["conv_b"][layer], w["w_gate_a"][layer],
            w["b_gate_a"][layer], w["w_gate_x"][layer], w["b_gate_x"][layer], w["rg_lambda"][layer],
            batch=batch, seq=seq, tc=t["tc"], blocks_per_step=t["per_step"],
            stream_start=stream_start)
        xf = _out_stage(yh, yr, proj, xf, w["w_branch_hgrn"][layer], w["w_branch_rglru"][layer],
                        w["w_out"][layer], w["post_norm_g"][layer], tm=t["tm_out"])
        new_s.append(s_l)
        new_conv.append(conv_l.transpose(0, 2, 1, 3).reshape(batch, CONV_W - 1, D_MODEL))
        new_h.append(h_l.reshape(batch, D_MODEL))
    return xf.reshape(batch, seq, D_MODEL), jnp.stack(new_s), jnp.stack(new_conv), jnp.stack(new_h)


def _prep_weights(lb_param, pre_norm_g, w_in, hgrn_norm_g, w_branch_hgrn, conv_w, conv_b, w_gate_a,
                  b_gate_a, w_gate_x, b_gate_x, rg_lambda, w_branch_rglru, w_out, post_norm_g):
    depth = w_in.shape[0]

    def rows(a):
        return a.reshape(depth, 1, D_MODEL)

    def blocks(a):
        return a.reshape(depth, RG_BLOCKS, 1, LANES)

    return dict(
        lbp=lb_param.astype(F32).reshape(depth, HEADS, LANES).transpose(1, 0, 2),
        pre_norm_g=rows(pre_norm_g), post_norm_g=rows(post_norm_g),
        hgrn_norm_g=hgrn_norm_g.reshape(depth, HEADS, 1, LANES),
        w_in=w_in.astype(BF16), w_branch_hgrn=w_branch_hgrn.astype(BF16),
        w_branch_rglru=w_branch_rglru.astype(BF16), w_out=w_out.astype(BF16),
        conv_w=conv_w.reshape(depth, CONV_W, RG_BLOCKS, LANES).transpose(0, 2, 1, 3),
        conv_b=blocks(conv_b), w_gate_a=w_gate_a.astype(BF16), b_gate_a=blocks(b_gate_a),
        w_gate_x=w_gate_x.astype(BF16), b_gate_x=blocks(b_gate_x), rg_lambda=blocks(rg_lambda),
    )


def kernel(x_prompt, x_sample, state_hgrn, state_conv, state_rglru, lb_param, pre_norm_g, w_in,
           hgrn_norm_g, w_branch_hgrn, conv_w, conv_b, w_gate_a, b_gate_a, w_gate_x, b_gate_x,
           rg_lambda, w_branch_rglru, w_out, post_norm_g):
    w = _prep_weights(lb_param, pre_norm_g, w_in, hgrn_norm_g, w_branch_hgrn, conv_w, conv_b,
                      w_gate_a, b_gate_a, w_gate_x, b_gate_x, rg_lambda, w_branch_rglru, w_out,
                      post_norm_g)
    depth = w_in.shape[0]
    nb = x_prompt.shape[0]
    s0 = jnp.zeros((depth, nb, HEADS, HEAD_DIM, HEAD_DIM), F32)
    c0 = jnp.zeros((depth, nb, CONV_W - 1, D_MODEL), F32)
    h0 = jnp.zeros((depth, nb, D_MODEL), F32)
    y_p, s_p, c_p, h_p = _run_trunk(x_prompt, s0, c0, h0, True, w)
    y_s, s_s, c_s, h_s = _run_trunk(x_sample, state_hgrn, state_conv, state_rglru, False, w)
    return (y_p, y_s, s_p, c_p, h_p, s_s, c_s, h_s)
```

```python
import functools

import jax
import jax.numpy as jnp
from jax import lax
from jax.experimental import pallas as pl
from jax.experimental.pallas import tpu as pltpu

F32 = jnp.float32
BF16 = jnp.bfloat16

D_MODEL = 2048
HEADS = 16
HEAD_DIM = 128
RG_BLOCKS = 16
CONV_W = 4
RG_C = 8.0
EPS = 1e-6
LANES = 128
SUBLANES = 8
PROJ = 8 * D_MODEL
N_COL_BLOCKS = PROJ // LANES
CB_Q, CB_F, CB_V, CB_GH, CB_XR, CB_GR, CB_MH, CB_MR = (16 * i for i in range(8))
HGRN_CHUNK = 64
HGRN_UNROLL = 16
LOG2E = 1.4426950408889634
HGRN_SPAN = 2
HGRN_SAFE_LOG2 = 120.0
VMEM_LIMIT_BYTES = 56 * 1024 * 1024


def _sigmoid_pair(z):
    e = jnp.exp(-jnp.abs(z))
    inv = 1.0 / (1.0 + e)
    pos = z >= 0
    sig_pos = jnp.where(pos, 1.0, e) * inv
    sig_neg = jnp.where(pos, e, 1.0) * inv
    log_sig = jnp.minimum(z, 0.0) - jnp.log1p(e)
    return sig_pos, sig_neg, log_sig


def _silu(x):
    return x * jax.nn.sigmoid(x)


def _in_proj_kernel(x_ref, g_ref, w_ref, o_ref, u_ref, *, tn):
    @pl.when(pl.program_id(1) == 0)
    def _():
        x = x_ref[...]
        ms = jnp.mean(x * x, axis=-1, keepdims=True)
        u_ref[...] = (x * lax.rsqrt(ms + EPS) * g_ref[...]).astype(BF16)

    res = jnp.dot(u_ref[...], w_ref[...], preferred_element_type=F32)
    for c in range(tn // LANES):
        o_ref[c] = res[:, c * LANES:(c + 1) * LANES]


def _in_proj(x, g, w, *, tm, tn):
    m = x.shape[0]
    return pl.pallas_call(
        functools.partial(_in_proj_kernel, tn=tn),
        grid=(m // tm, PROJ // tn),
        in_specs=[
            pl.BlockSpec((tm, D_MODEL), lambda i, j: (i, 0)),
            pl.BlockSpec((1, D_MODEL), lambda i, j: (0, 0)),
            pl.BlockSpec((D_MODEL, tn), lambda i, j: (0, j)),
        ],
        out_specs=pl.BlockSpec((tn // LANES, tm, LANES), lambda i, j: (j, i, 0)),
        out_shape=jax.ShapeDtypeStruct((N_COL_BLOCKS, m, LANES), F32),
        scratch_shapes=[pltpu.VMEM((tm, D_MODEL), BF16)],
        compiler_params=pltpu.CompilerParams(
            dimension_semantics=("parallel", "arbitrary"), vmem_limit_bytes=VMEM_LIMIT_BYTES),
        name="in_proj",
    )(x, g, w)


def _hgrn_prep(q_raw, z, lb, qkb_ref, shift_ok, *, chunk):
    n_groups = chunk // SUBLANES
    q = _silu(q_raw)
    sig_pos, sig_neg, log_sig = _sigmoid_pair(z)
    if lb is None:
        log_f, k = log_sig, sig_neg
    else:
        log_f = jnp.log(lb + (1.0 - lb) * sig_pos)
        k = (1.0 - lb) * sig_neg
    lf2 = log_f * LOG2E
    bs = []
    run = None
    for g in range(n_groups):
        lf = lf2[g * SUBLANES:(g + 1) * SUBLANES]
        bg = lf
        for ok, d in zip(shift_ok, (1, 2, 4)):
            bg = bg + jnp.where(ok, pltpu.roll(bg, d, 0), 0.0)
        tot = jnp.sum(lf, axis=0, keepdims=True)
        if run is not None:
            bg = bg + run
            tot = tot + run
        run = tot
        bs.append(bg)
    qkb_ref[0, SUBLANES:SUBLANES + chunk, :] = q
    qkb_ref[1, SUBLANES:SUBLANES + chunk, :] = k
    qkb_ref[2, SUBLANES:SUBLANES + chunk, :] = jnp.concatenate(bs, axis=0)
    return run


def _hgrn_scores_exact(qkb_ref, consts, *, chunk):
    diag_c, col_c = consts
    n_groups = chunk // SUBLANES

    def grp(plane, g, shift=0):
        lo = SUBLANES + g * SUBLANES - shift
        return qkb_ref[plane, lo:lo + SUBLANES, :]

    def end_row(g):
        lo = SUBLANES + g * SUBLANES + SUBLANES - 1
        return qkb_ref[2, lo:lo + 1, :]

    p_rows = []
    for g in range(n_groups):
        acc = jnp.zeros((SUBLANES, chunk), F32)
        for d in range(SUBLANES):
            a_d = grp(0, g) * grp(1, g, d)
            if d:
                a_d = a_d * jnp.exp2(grp(2, g) - grp(2, g, d))
            acc = jnp.where(diag_c == g * SUBLANES - d, jnp.sum(a_d, axis=-1, keepdims=True), acc)
        p_rows.append(jnp.where(col_c >= g * SUBLANES, acc, 0.0) if g else acc)

    half = chunk // 2
    while half >= SUBLANES:
        gph = half // SUBLANES
        q_parts, k_parts, q_groups = [], [], []
        for g in range(n_groups):
            first = (g // gph) * gph
            if (g // gph) % 2 == 1:
                q_parts.append(grp(0, g) * jnp.exp2(grp(2, g) - end_row(first - 1)))
                q_groups.append(g)
                k_parts.append(jnp.zeros((SUBLANES, LANES), F32))
            else:
                k_parts.append(grp(1, g) * jnp.exp2(end_row(first + gph - 1) - grp(2, g)))
        ql = jnp.concatenate(q_parts, axis=0).astype(BF16)
        kl = jnp.concatenate(k_parts, axis=0).astype(BF16)
        sc = lax.dot_general(ql, kl, (((1,), (1,)), ((), ())), preferred_element_type=F32)
        for j, g in enumerate(q_groups):
            lo = ((g // gph) - 1) * half
            valid = (col_c >= lo) & (col_c < lo + half)
            p_rows[g] = p_rows[g] + jnp.where(valid, sc[j * SUBLANES:(j + 1) * SUBLANES], 0.0)
        half //= 2
    return jnp.concatenate(p_rows, axis=0)


def _block_diag(blocks):
    n = len(blocks)
    zero = jnp.zeros_like(blocks[0])
    return jnp.concatenate(
        [jnp.concatenate([blk if j == i else zero for j in range(n)], axis=1)
         for i, blk in enumerate(blocks)], axis=0)


def _hgrn_finish(runs, st, ng):
    chunk = runs[0]["v"].shape[0] // len(runs[0]["qkb"])
    rows = slice(SUBLANES, SUBLANES + chunk)
    staged = []
    for r in runs:
        bs = [ref[2, rows, :] for ref in r["qkb"]]
        ks = [ref[1, rows, :] for ref in r["qkb"]]
        q_ins = [(ref[0, rows, :] * jnp.exp2(b)).astype(BF16) for ref, b in zip(r["qkb"], bs)]
        k_outs = [(k * jnp.exp2(e - b)).astype(BF16) for k, b, e in zip(ks, bs, r["b_ends"])]
        upd = jnp.dot(r["v"].T.astype(BF16), _block_diag(k_outs), preferred_element_type=F32)
        p_mat = r["scores_fn"](q_ins, ks, bs).astype(BF16)
        staged.append((q_ins, upd, p_mat))
    states = []
    for r, (q_ins, upd, p_mat) in zip(runs, staged):
        run_states = []
        for c, e in enumerate(r["b_ends"]):
            run_states.append(st)
            st = st * jnp.exp2(e) + upd[:, c * LANES:(c + 1) * LANES]
        states.append(jnp.concatenate(run_states, axis=1).astype(BF16))
    outs = []
    for r, (q_ins, upd, p_mat), s_cat in zip(runs, staged, states):
        o = jnp.dot(p_mat, r["v"].astype(BF16), preferred_element_type=F32)
        o = o + lax.dot_general(_block_diag(q_ins), s_cat, (((1,), (1,)), ((), ())),
                                preferred_element_type=F32)
        o = o * lax.rsqrt(jnp.mean(o * o, axis=-1, keepdims=True) + EPS)
        outs.append(o * ng * _silu(r["g_h"]))
    return outs, st


def _lower_bound(lbp, layer):
    if layer == 0:
        return None
    ex = jnp.exp(lbp - jnp.max(lbp, axis=0, keepdims=True))
    p = ex / jnp.sum(ex, axis=0, keepdims=True)
    return jnp.sum(p[1:layer + 1], axis=0, keepdims=True)


def _hgrn_kernel(lbp_ref, ng_ref, q_ref, f_ref, v_ref, g_ref, s0_ref, y_ref, s_out_ref, st_ref, qkb_ref,
                 *, layer, chunk, n_chunks, heads_per_step, unroll):
    tt = pl.program_id(2)

    @pl.when(tt == 0)
    def _():
        for hh in range(heads_per_step):
            st_ref[hh] = s0_ref[0, hh].T

    for u in range(unroll):
        qkb_ref[u, :, 0:SUBLANES, :] = jnp.zeros((3, SUBLANES, LANES), F32)
    sub = lax.broadcasted_iota(jnp.int32, (SUBLANES, LANES), 0)
    shift_ok = [sub >= d for d in (1, 2, 4)]
    n_iter = n_chunks // unroll
    span = min(HGRN_SPAN, unroll)

    def body(idx, carry):
        hh = idx // n_iter
        base = (idx % n_iter) * (unroll * chunk)
        lb = _lower_bound(lbp_ref[hh], layer)
        ng = ng_ref[hh]

        def rows(u):
            return pl.ds(pl.multiple_of(base + u * chunk, chunk), chunk)

        b_ends = [_hgrn_prep(q_ref[hh, rows(u), :], f_ref[hh, rows(u), :], lb, qkb_ref.at[u], shift_ok,
                             chunk=chunk) for u in range(unroll)]
        lowest = functools.reduce(jnp.minimum, b_ends)
        safe = jnp.min(lowest, axis=1, keepdims=True)[0, 0] >= -HGRN_SAFE_LOG2

        def run(scores_fn):
            def rr(u):
                return pl.ds(pl.multiple_of(base + u * chunk, chunk), span * chunk)
            starts = list(range(0, unroll, span))
            runs = [dict(qkb=[qkb_ref.at[u + c] for c in range(span)], b_ends=b_ends[u:u + span],
                         v=v_ref[hh, rr(u), :], g_h=g_ref[hh, rr(u), :], scores_fn=scores_fn(u))
                    for u in starts]
            outs, st = _hgrn_finish(runs, st_ref[hh], ng)
            for u, y in zip(starts, outs):
                y_ref[hh, rr(u), :] = y.astype(BF16)
            st_ref[hh] = st

        @pl.when(safe)
        def _():
            t_idx = lax.broadcasted_iota(jnp.int32, (span * chunk, span * chunk), 0)
            s_idx = lax.broadcasted_iota(jnp.int32, (span * chunk, span * chunk), 1)
            keep = s_idx <= t_idx
            for c in range(1, span):
                keep = keep & ((s_idx >= c * chunk) | (t_idx < c * chunk))

            def scores_fn(u):
                def fn(q_ins, ks, bs):
                    k_rel = jnp.concatenate([(k * jnp.exp2(-b)).astype(BF16) for k, b in zip(ks, bs)],
                                            axis=0)
                    sc = lax.dot_general(jnp.concatenate(q_ins, axis=0), k_rel,
                                         (((1,), (1,)), ((), ())), preferred_element_type=F32)
                    return jnp.where(keep, sc, 0.0)
                return fn
            run(scores_fn)

        @pl.when(jnp.logical_not(safe))
        def _():
            sub_c = lax.broadcasted_iota(jnp.int32, (SUBLANES, chunk), 0)
            col_c = lax.broadcasted_iota(jnp.int32, (SUBLANES, chunk), 1)
            consts = (col_c - sub_c, col_c)

            def scores_fn(u):
                return lambda q_ins, ks, bs: _block_diag(
                    [_hgrn_scores_exact(qkb_ref.at[u + c], consts, chunk=chunk) for c in range(span)])
            run(scores_fn)
        return carry

    lax.fori_loop(0, heads_per_step * n_iter, body, 0)

    @pl.when(tt == pl.num_programs(2) - 1)
    def _():
        for hh in range(heads_per_step):
            s_out_ref[0, hh] = st_ref[hh].T


def _hgrn(proj, lbp, ng, s0, *, layer, batch, seq, tc, chunk, heads_per_step):
    m = proj.shape[1]
    nt = seq // tc
    hb = heads_per_step
    unroll = min(HGRN_UNROLL, tc // chunk)

    def sec(cb):
        return pl.BlockSpec((hb, tc, LANES), lambda b, h, t: (cb // hb + h, b * nt + t, 0))

    return pl.pallas_call(
        functools.partial(_hgrn_kernel, layer=layer, chunk=chunk, n_chunks=tc // chunk,
                          heads_per_step=hb, unroll=unroll),
        grid=(batch, HEADS // hb, nt),
        in_specs=[
            pl.BlockSpec((hb,) + lbp.shape[1:], lambda b, h, t: (h, 0, 0)),
            pl.BlockSpec((hb, 1, LANES), lambda b, h, t: (h, 0, 0)),
            sec(CB_Q), sec(CB_F), sec(CB_V), sec(CB_GH),
            pl.BlockSpec((1, hb, HEAD_DIM, HEAD_DIM), lambda b, h, t: (b, h, 0, 0)),
        ],
        out_specs=[
            pl.BlockSpec((hb, tc, LANES), lambda b, h, t: (h, b * nt + t, 0)),
            pl.BlockSpec((1, hb, HEAD_DIM, HEAD_DIM), lambda b, h, t: (b, h, 0, 0)),
        ],
        out_shape=[
            jax.ShapeDtypeStruct((HEADS, m, LANES), BF16),
            jax.ShapeDtypeStruct((batch, HEADS, HEAD_DIM, HEAD_DIM), F32),
        ],
        scratch_shapes=[pltpu.VMEM((hb, HEAD_DIM, HEAD_DIM), F32),
                        pltpu.VMEM((unroll, 3, SUBLANES + chunk, LANES), F32)],
        compiler_params=pltpu.CompilerParams(
            dimension_semantics=("parallel", "parallel", "arbitrary"),
            vmem_limit_bytes=VMEM_LIMIT_BYTES),
        name="hgrn2",
    )(lbp, ng, proj, proj, proj, proj, s0)


def _rglru_kernel(xr_ref, gr_ref, hist_ref, h0_ref, cw_ref, cb_ref, wa_ref, ba_ref, wx_ref, bx_ref,
                  lam_ref, y_ref, conv_out_ref, h_out_ref, xbuf_ref, hcar_ref, a_ref, b_ref, h_ref,
                  *, tc, blocks_per_step, stream_start):
    tt = pl.program_id(2)
    last = tt == pl.num_programs(2) - 1
    hist_lo = SUBLANES - (CONV_W - 1)
    n_groups = tc // SUBLANES
    sub = lax.broadcasted_iota(jnp.int32, (SUBLANES, LANES), 0)
    row = lax.broadcasted_iota(jnp.int32, (tc, LANES), 0)

    @pl.when(tt == 0)
    def _():
        for jj in range(blocks_per_step):
            xbuf_ref[jj, hist_lo:SUBLANES, :] = hist_ref[0, jj]
            hcar_ref[jj] = jnp.broadcast_to(h0_ref[0, jj], (SUBLANES, LANES))

    def block_body(jj, carry):
        x = xr_ref[jj]
        xbuf_ref[jj, SUBLANES:SUBLANES + tc, :] = x
        cw = cw_ref[jj]
        xc = cb_ref[jj] + xbuf_ref[jj, pl.ds(hist_lo, tc), :] * cw[0:1, :]
        for j in range(1, CONV_W):
            xc = xc + xbuf_ref[jj, pl.ds(hist_lo + j, tc), :] * cw[j:j + 1, :]
        tail = x[tc - (CONV_W - 1):tc, :]
        xbuf_ref[jj, hist_lo:SUBLANES, :] = tail

        xcb = xc.astype(BF16)
        r = jax.nn.sigmoid(jnp.dot(xcb, wa_ref[jj], preferred_element_type=F32) + ba_ref[jj])
        gi = jax.nn.sigmoid(jnp.dot(xcb, wx_ref[jj], preferred_element_type=F32) + bx_ref[jj])
        nl = -lam_ref[jj]
        softplus_nl = jnp.maximum(nl, 0.0) + jnp.log1p(jnp.exp(-jnp.abs(nl)))
        log_a = -RG_C * r * softplus_nl
        a = jnp.exp(log_a)
        m2 = -jnp.tanh(log_a) * (a * a + 1.0)
        mult = jnp.where(m2 > 0.0, m2 * lax.rsqrt(m2), 0.0)
        if stream_start:
            mult = jnp.where((row == 0) & (tt == 0), 1.0, mult)
        a_ref[...] = a
        b_ref[...] = mult * gi * xc

        def scan_body(g, h_prev):
            rows = pl.ds(pl.multiple_of(g * SUBLANES, SUBLANES), SUBLANES)
            ag = a_ref[rows, :]
            bg = b_ref[rows, :]
            for d in (1, 2, 4):
                keep = sub >= d
                bg = ag * jnp.where(keep, pltpu.roll(bg, d, 0), 0.0) + bg
                ag = ag * jnp.where(keep, pltpu.roll(ag, d, 0), 1.0)
            hg = ag * h_prev + bg
            h_ref[rows, :] = hg
            return jnp.broadcast_to(hg[SUBLANES - 1:SUBLANES, :], (SUBLANES, LANES))

        h_last = lax.fori_loop(0, n_groups, scan_body, hcar_ref[jj], unroll=min(8, n_groups))
        hcar_ref[jj] = h_last
        y_ref[jj] = (h_ref[...] * _silu(gr_ref[jj])).astype(BF16)

        @pl.when(last)
        def _():
            conv_out_ref[0, jj] = tail
            h_out_ref[0, jj] = h_last[0:1, :]
        return carry

    lax.fori_loop(0, blocks_per_step, block_body, 0)


def _rglru(proj, hist, h0, cw, cb, wa, ba, wx, bx, lam, *, batch, seq, tc, blocks_per_step,
           stream_start):
    m = proj.shape[1]
    nt = seq // tc
    jb = blocks_per_step

    def sec(cb_off):
        return pl.BlockSpec((jb, tc, LANES), lambda b, j, t: (cb_off // jb + j, b * nt + t, 0))

    def per_block(shape):
        return pl.BlockSpec((jb,) + shape, lambda b, j, t: (j,) + (0,) * len(shape))

    return pl.pallas_call(
        functools.partial(_rglru_kernel, tc=tc, blocks_per_step=jb, stream_start=stream_start),
        grid=(batch, RG_BLOCKS // jb, nt),
        in_specs=[
            sec(CB_XR), sec(CB_GR),
            pl.BlockSpec((1, jb, CONV_W - 1, LANES), lambda b, j, t: (b, j, 0, 0)),
            pl.BlockSpec((1, jb, 1, LANES), lambda b, j, t: (b, j, 0, 0)),
            per_block((CONV_W, LANES)), per_block((1, LANES)),
            per_block((LANES, LANES)), per_block((1, LANES)),
            per_block((LANES, LANES)), per_block((1, LANES)),
            per_block((1, LANES)),
        ],
        out_specs=[
            pl.BlockSpec((jb, tc, LANES), lambda b, j, t: (j, b * nt + t, 0)),
            pl.BlockSpec((1, jb, CONV_W - 1, LANES), lambda b, j, t: (b, j, 0, 0)),
            pl.BlockSpec((1, jb, 1, LANES), lambda b, j, t: (b, j, 0, 0)),
        ],
        out_shape=[
            jax.ShapeDtypeStruct((RG_BLOCKS, m, LANES), BF16),
            jax.ShapeDtypeStruct((batch, RG_BLOCKS, CONV_W - 1, LANES), F32),
            jax.ShapeDtypeStruct((batch, RG_BLOCKS, 1, LANES), F32),
        ],
        scratch_shapes=[
            pltpu.VMEM((jb, SUBLANES + tc, LANES), F32),
            pltpu.VMEM((jb, SUBLANES, LANES), F32),
            pltpu.VMEM((tc, LANES), F32),
            pltpu.VMEM((tc, LANES), F32),
            pltpu.VMEM((tc, LANES), F32),
        ],
        compiler_params=pltpu.CompilerParams(
            dimension_semantics=("parallel", "parallel", "arbitrary"),
            vmem_limit_bytes=VMEM_LIMIT_BYTES),
        name="rglru",
    )(proj, proj, hist, h0, cw, cb, wa, ba, wx, bx, lam)


def _out_kernel(yh_ref, yr_ref, mh_ref, mr_ref, x_ref, wbh_ref, wbr_ref, wo_ref, g_ref, o_ref):
    def gather(ref):
        return jnp.concatenate([ref[c] for c in range(D_MODEL // LANES)], axis=1)

    b_h = jnp.dot(gather(yh_ref), wbh_ref[...], preferred_element_type=F32)
    b_r = jnp.dot(gather(yr_ref), wbr_ref[...], preferred_element_type=F32)
    merged = jax.nn.sigmoid(gather(mh_ref)) * b_h + jax.nn.sigmoid(gather(mr_ref)) * b_r
    out = jnp.dot(merged.astype(BF16), wo_ref[...], preferred_element_type=F32)
    ms = jnp.mean(out * out, axis=-1, keepdims=True)
    o_ref[...] = x_ref[...] + out * lax.rsqrt(ms + EPS) * g_ref[...]


def _out_stage(yh, yr, proj, x, wbh, wbr, wo, g, *, tm):
    m = x.shape[0]
    nb = D_MODEL // LANES

    def resident(shape):
        return pl.BlockSpec(shape, lambda i: (0,) * len(shape), pipeline_mode=pl.Buffered(1))

    return pl.pallas_call(
        _out_kernel,
        grid=(m // tm,),
        in_specs=[
            pl.BlockSpec((nb, tm, LANES), lambda i: (0, i, 0)),
            pl.BlockSpec((nb, tm, LANES), lambda i: (0, i, 0)),
            pl.BlockSpec((nb, tm, LANES), lambda i: (CB_MH // nb, i, 0)),
            pl.BlockSpec((nb, tm, LANES), lambda i: (CB_MR // nb, i, 0)),
            pl.BlockSpec((tm, D_MODEL), lambda i: (i, 0)),
            resident((D_MODEL, D_MODEL)), resident((D_MODEL, D_MODEL)), resident((D_MODEL, D_MODEL)),
            resident((1, D_MODEL)),
        ],
        out_specs=pl.BlockSpec((tm, D_MODEL), lambda i: (i, 0)),
        out_shape=jax.ShapeDtypeStruct((m, D_MODEL), F32),
        compiler_params=pltpu.CompilerParams(
            dimension_semantics=("parallel",), vmem_limit_bytes=VMEM_LIMIT_BYTES),
        name="out_stage",
    )(yh, yr, proj, proj, x, wbh, wbr, wo, g)


def _stream_tiles(batch, seq):
    m = batch * seq
    tm_in = min(m, 1024)
    tm_out = min(m, 256)
    tc = min(seq, 1024)
    chunk = min(seq, HGRN_CHUNK)
    per_step = 1 if seq >= 1024 else HEADS
    return dict(tm_in=tm_in, tm_out=tm_out, tc=tc, chunk=chunk, per_step=per_step)


def _run_trunk(x, s_hgrn, conv_hist, h_rg, stream_start, w):
    batch, seq, _ = x.shape
    m = batch * seq
    t = _stream_tiles(batch, seq)
    xf = x.reshape(m, D_MODEL)
    new_s, new_conv, new_h = [], [], []
    for layer in range(len(w["w_in"])):
        proj = _in_proj(xf, w["pre_norm_g"][layer], w["w_in"][layer], tm=t["tm_in"], tn=1024)
        yh, s_l = _hgrn(proj, w["lbp"], w["hgrn_norm_g"][layer], s_hgrn[layer], layer=layer,
                        batch=batch, seq=seq, tc=t["tc"], chunk=t["chunk"],
                        heads_per_step=t["per_step"])
        hist = conv_hist[layer].reshape(batch, CONV_W - 1, RG_BLOCKS, LANES).transpose(0, 2, 1, 3)
        h0 = h_rg[layer].reshape(batch, RG_BLOCKS, 1, LANES)
        yr, conv_l, h_l = _rglru(
            proj, hist, h0, w["conv_w"][layer], w["conv_b"][layer], w["w_gate_a"][layer],
            w["b_gate_a"][layer], w["w_gate_x"][layer], w["b_gate_x"][layer], w["rg_lambda"][layer],
            batch=batch, seq=seq, tc=t["tc"], blocks_per_step=t["per_step"],
            stream_start=stream_start)
        xf = _out_stage(yh, yr, proj, xf, w["w_branch_hgrn"][layer], w["w_branch_rglru"][layer],
                        w["w_out"][layer], w["post_norm_g"][layer], tm=t["tm_out"])
        new_s.append(s_l)
        new_conv.append(conv_l.transpose(0, 2, 1, 3).reshape(batch, CONV_W - 1, D_MODEL))
        new_h.append(h_l.reshape(batch, D_MODEL))
    return xf.reshape(batch, seq, D_MODEL), jnp.stack(new_s), jnp.stack(new_conv), jnp.stack(new_h)


def _prep_weights(lb_param, pre_norm_g, w_in, hgrn_norm_g, w_branch_hgrn, conv_w, conv_b, w_gate_a,
                  b_gate_a, w_gate_x, b_gate_x, rg_lambda, w_branch_rglru, w_out, post_norm_g):
    depth = w_in.shape[0]

    def rows(a):
        return a.reshape(depth, 1, D_MODEL)

    def blocks(a):
        return a.reshape(depth, RG_BLOCKS, 1, LANES)

    return dict(
        lbp=lb_param.astype(F32).reshape(depth, HEADS, LANES).transpose(1, 0, 2),
        pre_norm_g=rows(pre_norm_g), post_norm_g=rows(post_norm_g),
        hgrn_norm_g=hgrn_norm_g.reshape(depth, HEADS, 1, LANES),
        w_in=w_in.astype(BF16), w_branch_hgrn=w_branch_hgrn.astype(BF16),
        w_branch_rglru=w_branch_rglru.astype(BF16), w_out=w_out.astype(BF16),
        conv_w=conv_w.reshape(depth, CONV_W, RG_BLOCKS, LANES).transpose(0, 2, 1, 3),
        conv_b=blocks(conv_b), w_gate_a=w_gate_a.astype(BF16), b_gate_a=blocks(b_gate_a),
        w_gate_x=w_gate_x.astype(BF16), b_gate_x=blocks(b_gate_x), rg_lambda=blocks(rg_lambda),
    )


def kernel(x_prompt, x_sample, state_hgrn, state_conv, state_rglru, lb_param, pre_norm_g, w_in,
           hgrn_norm_g, w_branch_hgrn, conv_w, conv_b, w_gate_a, b_gate_a, w_gate_x, b_gate_x,
           rg_lambda, w_branch_rglru, w_out, post_norm_g):
    w = _prep_weights(lb_param, pre_norm_g, w_in, hgrn_norm_g, w_branch_hgrn, conv_w, conv_b,
                      w_gate_a, b_gate_a, w_gate_x, b_gate_x, rg_lambda, w_branch_rglru, w_out,
                      post_norm_g)
    depth = w_in.shape[0]
    nb = x_prompt.shape[0]
    s0 = jnp.zeros((depth, nb, HEADS, HEAD_DIM, HEAD_DIM), F32)
    c0 = jnp.zeros((depth, nb, CONV_W - 1, D_MODEL), F32)
    h0 = jnp.zeros((depth, nb, D_MODEL), F32)
    y_p, s_p, c_p, h_p = _run_trunk(x_prompt, s0, c0, h0, True, w)
    y_s, s_s, c_s, h_s = _run_trunk(x_sample, state_hgrn, state_conv, state_rglru, False, w)
    return (y_p, y_s, s_p, c_p, h_p, s_s, c_s, h_s)
```

```python
import functools

import jax
import jax.numpy as jnp
from jax import lax
from jax.experimental import pallas as pl
from jax.experimental.pallas import tpu as pltpu

F32 = jnp.float32
BF16 = jnp.bfloat16

D_MODEL = 2048
HEADS = 16
HEAD_DIM = 128
RG_BLOCKS = 16
CONV_W = 4
RG_C = 8.0
EPS = 1e-6
LANES = 128
SUBLANES = 8
PROJ = 8 * D_MODEL
N_COL_BLOCKS = PROJ // LANES
CB_Q, CB_F, CB_V, CB_GH, CB_XR, CB_GR, CB_MH, CB_MR = (16 * i for i in range(8))
HGRN_CHUNK = 64
HGRN_UNROLL = 16
LOG2E = 1.4426950408889634
HGRN_HEADS_UNROLL = 8
HGRN_SPAN = 2
HGRN_SAFE_LOG2 = 120.0
SCAN_VREGS = 4
SCAN_PAD = 4
VMEM_LIMIT_BYTES = 56 * 1024 * 1024


def _sigmoid_pair(z):
    e = jnp.exp(-jnp.abs(z))
    inv = 1.0 / (1.0 + e)
    pos = z >= 0
    sig_pos = jnp.where(pos, 1.0, e) * inv
    sig_neg = jnp.where(pos, e, 1.0) * inv
    log_sig = jnp.minimum(z, 0.0) - jnp.log1p(e)
    return sig_pos, sig_neg, log_sig


def _silu(x):
    return x * jax.nn.sigmoid(x)


def _in_proj_kernel(x_ref, g_ref, w_ref, o_ref, u_ref, *, tn):
    @pl.when(pl.program_id(1) == 0)
    def _():
        x = x_ref[...]
        ms = jnp.mean(x * x, axis=-1, keepdims=True)
        u_ref[...] = (x * lax.rsqrt(ms + EPS) * g_ref[...]).astype(BF16)

    res = jnp.dot(u_ref[...], w_ref[...], preferred_element_type=F32)
    for c in range(tn // LANES):
        o_ref[c] = res[:, c * LANES:(c + 1) * LANES]


def _in_proj(x, g, w, *, tm, tn):
    m = x.shape[0]
    return pl.pallas_call(
        functools.partial(_in_proj_kernel, tn=tn),
        grid=(m // tm, PROJ // tn),
        in_specs=[
            pl.BlockSpec((tm, D_MODEL), lambda i, j: (i, 0)),
            pl.BlockSpec((1, D_MODEL), lambda i, j: (0, 0)),
            pl.BlockSpec((D_MODEL, tn), lambda i, j: (0, j)),
        ],
        out_specs=pl.BlockSpec((tn // LANES, tm, LANES), lambda i, j: (j, i, 0)),
        out_shape=jax.ShapeDtypeStruct((N_COL_BLOCKS, m, LANES), F32),
        scratch_shapes=[pltpu.VMEM((tm, D_MODEL), BF16)],
        compiler_params=pltpu.CompilerParams(
            dimension_semantics=("parallel", "arbitrary"), vmem_limit_bytes=VMEM_LIMIT_BYTES),
        name="in_proj",
    )(x, g, w)


def _hgrn_prep(q_raw, z, lb, qkb_ref, shift_ok, *, chunk):
    n_groups = chunk // SUBLANES
    q = _silu(q_raw)
    sig_pos, sig_neg, log_sig = _sigmoid_pair(z)
    if lb is None:
        log_f, k = log_sig, sig_neg
    else:
        log_f = jnp.log(lb + (1.0 - lb) * sig_pos)
        k = (1.0 - lb) * sig_neg
    lf2 = log_f * LOG2E
    bs = []
    run = None
    for g in range(n_groups):
        lf = lf2[g * SUBLANES:(g + 1) * SUBLANES]
        bg = lf
        for ok, d in zip(shift_ok, (1, 2, 4)):
            bg = bg + jnp.where(ok, pltpu.roll(bg, d, 0), 0.0)
        tot = jnp.sum(lf, axis=0, keepdims=True)
        if run is not None:
            bg = bg + run
            tot = tot + run
        run = tot
        bs.append(bg)
    qkb_ref[0, SUBLANES:SUBLANES + chunk, :] = q
    qkb_ref[1, SUBLANES:SUBLANES + chunk, :] = k
    qkb_ref[2, SUBLANES:SUBLANES + chunk, :] = jnp.concatenate(bs, axis=0)
    return run


def _hgrn_scores_exact(qkb_ref, consts, *, chunk):
    diag_c, col_c = consts
    n_groups = chunk // SUBLANES

    def grp(plane, g, shift=0):
        lo = SUBLANES + g * SUBLANES - shift
        return qkb_ref[plane, lo:lo + SUBLANES, :]

    def end_row(g):
        lo = SUBLANES + g * SUBLANES + SUBLANES - 1
        return qkb_ref[2, lo:lo + 1, :]

    p_rows = []
    for g in range(n_groups):
        acc = jnp.zeros((SUBLANES, chunk), F32)
        for d in range(SUBLANES):
            a_d = grp(0, g) * grp(1, g, d)
            if d:
                a_d = a_d * jnp.exp2(grp(2, g) - grp(2, g, d))
            acc = jnp.where(diag_c == g * SUBLANES - d, jnp.sum(a_d, axis=-1, keepdims=True), acc)
        p_rows.append(jnp.where(col_c >= g * SUBLANES, acc, 0.0) if g else acc)

    half = chunk // 2
    while half >= SUBLANES:
        gph = half // SUBLANES
        q_parts, k_parts, q_groups = [], [], []
        for g in range(n_groups):
            first = (g // gph) * gph
            if (g // gph) % 2 == 1:
                q_parts.append(grp(0, g) * jnp.exp2(grp(2, g) - end_row(first - 1)))
                q_groups.append(g)
                k_parts.append(jnp.zeros((SUBLANES, LANES), F32))
            else:
                k_parts.append(grp(1, g) * jnp.exp2(end_row(first + gph - 1) - grp(2, g)))
        ql = jnp.concatenate(q_parts, axis=0).astype(BF16)
        kl = jnp.concatenate(k_parts, axis=0).astype(BF16)
        sc = lax.dot_general(ql, kl, (((1,), (1,)), ((), ())), preferred_element_type=F32)
        for j, g in enumerate(q_groups):
            lo = ((g // gph) - 1) * half
            valid = (col_c >= lo) & (col_c < lo + half)
            p_rows[g] = p_rows[g] + jnp.where(valid, sc[j * SUBLANES:(j + 1) * SUBLANES], 0.0)
        half //= 2
    return jnp.concatenate(p_rows, axis=0)


def _block_diag(blocks):
    n = len(blocks)
    zero = jnp.zeros_like(blocks[0])
    return jnp.concatenate(
        [jnp.concatenate([blk if j == i else zero for j in range(n)], axis=1)
         for i, blk in enumerate(blocks)], axis=0)


def _hgrn_finish(runs):
    chunk = runs[0]["v"].shape[0] // len(runs[0]["qkb"])
    rows = slice(SUBLANES, SUBLANES + chunk)
    staged = []
    for r in runs:
        bs = [ref[2, rows, :] for ref in r["qkb"]]
        ks = [ref[1, rows, :] for ref in r["qkb"]]
        q_ins = [(ref[0, rows, :] * jnp.exp2(b)).astype(BF16) for ref, b in zip(r["qkb"], bs)]
        k_outs = [(k * jnp.exp2(e - b)).astype(BF16) for k, b, e in zip(ks, bs, r["b_ends"])]
        upd = jnp.dot(r["v"].T.astype(BF16), _block_diag(k_outs), preferred_element_type=F32)
        p_mat = r["scores_fn"](q_ins, ks, bs).astype(BF16)
        staged.append((q_ins, upd, p_mat))
    states, finals = [], []
    st = None
    for r, (q_ins, upd, p_mat) in zip(runs, staged):
        st = st if r["st"] is None else r["st"]
        run_states = []
        for c, e in enumerate(r["b_ends"]):
            run_states.append(st)
            st = st * jnp.exp2(e) + upd[:, c * LANES:(c + 1) * LANES]
        states.append(jnp.concatenate(run_states, axis=1).astype(BF16))
        finals.append(st)
    outs = []
    for r, (q_ins, upd, p_mat), s_cat in zip(runs, staged, states):
        o = jnp.dot(p_mat, r["v"].astype(BF16), preferred_element_type=F32)
        o = o + lax.dot_general(_block_diag(q_ins), s_cat, (((1,), (1,)), ((), ())),
                                preferred_element_type=F32)
        o = o * lax.rsqrt(jnp.mean(o * o, axis=-1, keepdims=True) + EPS)
        outs.append(o * r["ng"] * _silu(r["g_h"]))
    return outs, finals


def _lower_bound(lbp, layer):
    if layer == 0:
        return None
    ex = jnp.exp(lbp - jnp.max(lbp, axis=0, keepdims=True))
    p = ex / jnp.sum(ex, axis=0, keepdims=True)
    return jnp.sum(p[1:layer + 1], axis=0, keepdims=True)


def _hgrn_kernel(lbp_ref, ng_ref, q_ref, f_ref, v_ref, g_ref, s0_ref, y_ref, s_out_ref, st_ref, qkb_ref,
                 *, layer, chunk, n_chunks, heads_per_step, unroll):
    tt = pl.program_id(2)

    @pl.when(tt == 0)
    def _():
        for hh in range(heads_per_step):
            st_ref[hh] = s0_ref[0, hh].T

    for u in range(unroll):
        qkb_ref[u, :, 0:SUBLANES, :] = jnp.zeros((3, SUBLANES, LANES), F32)
    sub = lax.broadcasted_iota(jnp.int32, (SUBLANES, LANES), 0)
    shift_ok = [sub >= d for d in (1, 2, 4)]
    across_heads = n_chunks == 1
    span = 1 if across_heads else min(HGRN_SPAN, unroll)
    n_iter = (heads_per_step if across_heads else n_chunks) // unroll

    def body(idx, carry):
        if across_heads:
            heads = [idx * unroll + u for u in range(unroll)]
            base = 0
        else:
            heads = [idx // n_iter] * unroll
            base = (idx % n_iter) * (unroll * chunk)

        def rows(u, n=1):
            start = 0 if across_heads else pl.multiple_of(base + u * chunk, chunk)
            return pl.ds(start, n * chunk)

        b_ends = [_hgrn_prep(q_ref[heads[u], rows(u), :], f_ref[heads[u], rows(u), :],
                             _lower_bound(lbp_ref[heads[u]], layer), qkb_ref.at[u], shift_ok,
                             chunk=chunk) for u in range(unroll)]
        lowest = functools.reduce(jnp.minimum, b_ends)
        safe = jnp.min(lowest, axis=1, keepdims=True)[0, 0] >= -HGRN_SAFE_LOG2

        def run(scores_fn):
            starts = list(range(0, unroll, span))
            runs = [dict(qkb=[qkb_ref.at[u + c] for c in range(span)], b_ends=b_ends[u:u + span],
                         v=v_ref[heads[u], rows(u, span), :], g_h=g_ref[heads[u], rows(u, span), :],
                         ng=ng_ref[heads[u]], scores_fn=scores_fn(u),
                         st=st_ref[heads[u]] if (across_heads or u == 0) else None)
                    for u in starts]
            outs, finals = _hgrn_finish(runs)
            for u, y in zip(starts, outs):
                y_ref[heads[u], rows(u, span), :] = y.astype(BF16)
            if across_heads:
                for u, st in zip(starts, finals):
                    st_ref[heads[u]] = st
            else:
                st_ref[heads[0]] = finals[-1]

        @pl.when(safe)
        def _():
            t_idx = lax.broadcasted_iota(jnp.int32, (span * chunk, span * chunk), 0)
            s_idx = lax.broadcasted_iota(jnp.int32, (span * chunk, span * chunk), 1)
            keep = s_idx <= t_idx
            for c in range(1, span):
                keep = keep & ((s_idx >= c * chunk) | (t_idx < c * chunk))

            def scores_fn(u):
                def fn(q_ins, ks, bs):
                    k_rel = jnp.concatenate([(k * jnp.exp2(-b)).astype(BF16) for k, b in zip(ks, bs)],
                                            axis=0)
                    sc = lax.dot_general(jnp.concatenate(q_ins, axis=0), k_rel,
                                         (((1,), (1,)), ((), ())), preferred_element_type=F32)
                    return jnp.where(keep, sc, 0.0)
                return fn
            run(scores_fn)

        @pl.when(jnp.logical_not(safe))
        def _():
            sub_c = lax.broadcasted_iota(jnp.int32, (SUBLANES, chunk), 0)
            col_c = lax.broadcasted_iota(jnp.int32, (SUBLANES, chunk), 1)
            consts = (col_c - sub_c, col_c)

            def scores_fn(u):
                return lambda q_ins, ks, bs: _block_diag(
                    [_hgrn_scores_exact(qkb_ref.at[u + c], consts, chunk=chunk) for c in range(span)])
            run(scores_fn)
        return carry

    lax.fori_loop(0, n_iter if across_heads else heads_per_step * n_iter, body, 0)

    @pl.when(tt == pl.num_programs(2) - 1)
    def _():
        for hh in range(heads_per_step):
            s_out_ref[0, hh] = st_ref[hh].T


def _hgrn(proj, lbp, ng, s0, *, layer, batch, seq, tc, chunk, heads_per_step):
    m = proj.shape[1]
    nt = seq // tc
    hb = heads_per_step
    n_chunks = tc // chunk
    unroll = min(HGRN_UNROLL, n_chunks) if n_chunks > 1 else min(HGRN_HEADS_UNROLL, hb)

    def sec(cb):
        return pl.BlockSpec((hb, tc, LANES), lambda b, h, t: (cb // hb + h, b * nt + t, 0))

    return pl.pallas_call(
        functools.partial(_hgrn_kernel, layer=layer, chunk=chunk, n_chunks=tc // chunk,
                          heads_per_step=hb, unroll=unroll),
        grid=(batch, HEADS // hb, nt),
        in_specs=[
            pl.BlockSpec((hb,) + lbp.shape[1:], lambda b, h, t: (h, 0, 0)),
            pl.BlockSpec((hb, 1, LANES), lambda b, h, t: (h, 0, 0)),
            sec(CB_Q), sec(CB_F), sec(CB_V), sec(CB_GH),
            pl.BlockSpec((1, hb, HEAD_DIM, HEAD_DIM), lambda b, h, t: (b, h, 0, 0)),
        ],
        out_specs=[
            pl.BlockSpec((hb, tc, LANES), lambda b, h, t: (h, b * nt + t, 0)),
            pl.BlockSpec((1, hb, HEAD_DIM, HEAD_DIM), lambda b, h, t: (b, h, 0, 0)),
        ],
        out_shape=[
            jax.ShapeDtypeStruct((HEADS, m, LANES), BF16),
            jax.ShapeDtypeStruct((batch, HEADS, HEAD_DIM, HEAD_DIM), F32),
        ],
        scratch_shapes=[pltpu.VMEM((hb, HEAD_DIM, HEAD_DIM), F32),
                        pltpu.VMEM((unroll, 3, SUBLANES + chunk, LANES), F32)],
        compiler_params=pltpu.CompilerParams(
            dimension_semantics=("parallel", "parallel", "arbitrary"),
            vmem_limit_bytes=VMEM_LIMIT_BYTES),
        name="hgrn2",
    )(lbp, ng, proj, proj, proj, proj, s0)


def _rglru_kernel(xr_ref, gr_ref, hist_ref, h0_ref, cw_ref, cb_ref, wa_ref, ba_ref, wx_ref, bx_ref,
                  lam_ref, y_ref, conv_out_ref, h_out_ref, xbuf_ref, hcar_ref, a_ref, b_ref, p_ref, l_ref,
                  *, tc, blocks_per_step, stream_start, n_vregs):
    tt = pl.program_id(2)
    last = tt == pl.num_programs(2) - 1
    hist_lo = SUBLANES - (CONV_W - 1)
    n_seg = SUBLANES * n_vregs
    seg = tc // n_seg
    pitch = seg + SCAN_PAD
    sub = lax.broadcasted_iota(jnp.int32, (SUBLANES, LANES), 0)
    row = lax.broadcasted_iota(jnp.int32, (tc, LANES), 0)

    @pl.when(tt == 0)
    def _():
        for jj in range(blocks_per_step):
            xbuf_ref[jj, hist_lo:SUBLANES, :] = hist_ref[0, jj]
            hcar_ref[jj] = jnp.broadcast_to(h0_ref[0, jj], (SUBLANES, LANES))

    def block_body(jj, carry):
        x = xr_ref[jj]
        xbuf_ref[jj, SUBLANES:SUBLANES + tc, :] = x
        cw = cw_ref[jj]
        xc = cb_ref[jj] + xbuf_ref[jj, pl.ds(hist_lo, tc), :] * cw[0:1, :]
        for j in range(1, CONV_W):
            xc = xc + xbuf_ref[jj, pl.ds(hist_lo + j, tc), :] * cw[j:j + 1, :]
        tail = x[tc - (CONV_W - 1):tc, :]
        xbuf_ref[jj, hist_lo:SUBLANES, :] = tail

        xcb = xc.astype(BF16)
        r = jax.nn.sigmoid(jnp.dot(xcb, wa_ref[jj], preferred_element_type=F32) + ba_ref[jj])
        gi = jax.nn.sigmoid(jnp.dot(xcb, wx_ref[jj], preferred_element_type=F32) + bx_ref[jj])
        nl = -lam_ref[jj]
        rate = -RG_C * (jnp.maximum(nl, 0.0) + jnp.log1p(jnp.exp(-jnp.abs(nl))))
        log_a = r * rate
        a = jnp.exp2(r * (rate * LOG2E))
        m2 = -jnp.tanh(log_a) * (a * a + 1.0)
        mult = jnp.where(m2 > 0.0, m2 * lax.rsqrt(m2), 0.0)
        if stream_start:
            mult = jnp.where((row == 0) & (tt == 0), 1.0, mult)
        bt = mult * gi * xc
        for s in range(n_seg):
            a_ref[s * pitch:s * pitch + seg, :] = a[s * seg:(s + 1) * seg]
            b_ref[s * pitch:s * pitch + seg, :] = bt[s * seg:(s + 1) * seg]

        def scan_body(i, pl_prev):
            out = []
            for v, (p_prev, l_prev) in enumerate(pl_prev):
                rows = pl.ds(i + v * SUBLANES * pitch, SUBLANES, stride=pitch)
                ai = a_ref[rows, :]
                l_new = ai * l_prev + b_ref[rows, :]
                p_new = ai * p_prev
                p_ref[rows, :] = p_new
                l_ref[rows, :] = l_new
                out.append((p_new, l_new))
            return tuple(out)

        init = tuple((jnp.ones((SUBLANES, LANES), F32), jnp.zeros((SUBLANES, LANES), F32))
                     for _ in range(n_vregs))
        ends = lax.fori_loop(0, seg, scan_body, init, unroll=min(8, seg))
        h_in = hcar_ref[jj]
        pieces = []
        for v, (p_end, l_end) in enumerate(ends):
            for d in (1, 2, 4):
                keep = sub >= d
                l_end = p_end * jnp.where(keep, pltpu.roll(l_end, d, 0), 0.0) + l_end
                p_end = p_end * jnp.where(keep, pltpu.roll(p_end, d, 0), 1.0)
            h_seg_end = p_end * h_in + l_end
            h_seg_start = jnp.where(sub >= 1, pltpu.roll(h_seg_end, 1, 0), h_in)
            h_in = jnp.broadcast_to(h_seg_end[SUBLANES - 1:SUBLANES, :], (SUBLANES, LANES))
            for r in range(SUBLANES):
                lo = (v * SUBLANES + r) * pitch
                pieces.append(p_ref[lo:lo + seg, :] * h_seg_start[r:r + 1, :] + l_ref[lo:lo + seg, :])
        h_last = h_in
        hcar_ref[jj] = h_last
        h = jnp.concatenate(pieces, axis=0)
        y_ref[jj] = (h * _silu(gr_ref[jj])).astype(BF16)

        @pl.when(last)
        def _():
            conv_out_ref[0, jj] = tail
            h_out_ref[0, jj] = h_last[0:1, :]
        return carry

    lax.fori_loop(0, blocks_per_step, block_body, 0)


def _rglru(proj, hist, h0, cw, cb, wa, ba, wx, bx, lam, *, batch, seq, tc, blocks_per_step,
           stream_start):
    m = proj.shape[1]
    nt = seq // tc
    jb = blocks_per_step
    n_vregs = min(SCAN_VREGS, tc // SUBLANES)

    def sec(cb_off):
        return pl.BlockSpec((jb, tc, LANES), lambda b, j, t: (cb_off // jb + j, b * nt + t, 0))

    def per_block(shape):
        return pl.BlockSpec((jb,) + shape, lambda b, j, t: (j,) + (0,) * len(shape))

    return pl.pallas_call(
        functools.partial(_rglru_kernel, tc=tc, blocks_per_step=jb, stream_start=stream_start,
                          n_vregs=n_vregs),
        grid=(batch, RG_BLOCKS // jb, nt),
        in_specs=[
            sec(CB_XR), sec(CB_GR),
            pl.BlockSpec((1, jb, CONV_W - 1, LANES), lambda b, j, t: (b, j, 0, 0)),
            pl.BlockSpec((1, jb, 1, LANES), lambda b, j, t: (b, j, 0, 0)),
            per_block((CONV_W, LANES)), per_block((1, LANES)),
            per_block((LANES, LANES)), per_block((1, LANES)),
            per_block((LANES, LANES)), per_block((1, LANES)),
            per_block((1, LANES)),
        ],
        out_specs=[
            pl.BlockSpec((jb, tc, LANES), lambda b, j, t: (j, b * nt + t, 0)),
            pl.BlockSpec((1, jb, CONV_W - 1, LANES), lambda b, j, t: (b, j, 0, 0)),
            pl.BlockSpec((1, jb, 1, LANES), lambda b, j, t: (b, j, 0, 0)),
        ],
        out_shape=[
            jax.ShapeDtypeStruct((RG_BLOCKS, m, LANES), BF16),
            jax.ShapeDtypeStruct((batch, RG_BLOCKS, CONV_W - 1, LANES), F32),
            jax.ShapeDtypeStruct((batch, RG_BLOCKS, 1, LANES), F32),
        ],
        scratch_shapes=[
            pltpu.VMEM((jb, SUBLANES + tc, LANES), F32),
            pltpu.VMEM((jb, SUBLANES, LANES), F32),
        ] + [pltpu.VMEM((tc + SUBLANES * n_vregs * SCAN_PAD, LANES), F32)] * 4,
        compiler_params=pltpu.CompilerParams(
            dimension_semantics=("parallel", "parallel", "arbitrary"),
            vmem_limit_bytes=VMEM_LIMIT_BYTES),
        name="rglru",
    )(proj, proj, hist, h0, cw, cb, wa, ba, wx, bx, lam)


def _out_kernel(yh_ref, yr_ref, mh_ref, mr_ref, x_ref, wbh_ref, wbr_ref, wo_ref, g_ref, o_ref):
    def gather(ref):
        return jnp.concatenate([ref[c] for c in range(D_MODEL // LANES)], axis=1)

    b_h = jnp.dot(gather(yh_ref), wbh_ref[...], preferred_element_type=F32)
    b_r = jnp.dot(gather(yr_ref), wbr_ref[...], preferred_element_type=F32)
    merged = jax.nn.sigmoid(gather(mh_ref)) * b_h + jax.nn.sigmoid(gather(mr_ref)) * b_r
    out = jnp.dot(merged.astype(BF16), wo_ref[...], preferred_element_type=F32)
    ms = jnp.mean(out * out, axis=-1, keepdims=True)
    o_ref[...] = x_ref[...] + out * lax.rsqrt(ms + EPS) * g_ref[...]


def _out_stage(yh, yr, proj, x, wbh, wbr, wo, g, *, tm):
    m = x.shape[0]
    nb = D_MODEL // LANES

    def resident(shape):
        return pl.BlockSpec(shape, lambda i: (0,) * len(shape), pipeline_mode=pl.Buffered(1))

    return pl.pallas_call(
        _out_kernel,
        grid=(m // tm,),
        in_specs=[
            pl.BlockSpec((nb, tm, LANES), lambda i: (0, i, 0)),
            pl.BlockSpec((nb, tm, LANES), lambda i: (0, i, 0)),
            pl.BlockSpec((nb, tm, LANES), lambda i: (CB_MH // nb, i, 0)),
            pl.BlockSpec((nb, tm, LANES), lambda i: (CB_MR // nb, i, 0)),
            pl.BlockSpec((tm, D_MODEL), lambda i: (i, 0)),
            resident((D_MODEL, D_MODEL)), resident((D_MODEL, D_MODEL)), resident((D_MODEL, D_MODEL)),
            resident((1, D_MODEL)),
        ],
        out_specs=pl.BlockSpec((tm, D_MODEL), lambda i: (i, 0)),
        out_shape=jax.ShapeDtypeStruct((m, D_MODEL), F32),
        compiler_params=pltpu.CompilerParams(
            dimension_semantics=("parallel",), vmem_limit_bytes=VMEM_LIMIT_BYTES),
        name="out_stage",
    )(yh, yr, proj, proj, x, wbh, wbr, wo, g)


def _stream_tiles(batch, seq):
    m = batch * seq
    tm_in = min(m, 1024)
    tm_out = min(m, 256)
    tc = min(seq, 1024)
    chunk = min(seq, HGRN_CHUNK)
    per_step = 1 if seq >= 1024 else HEADS
    return dict(tm_in=tm_in, tm_out=tm_out, tc=tc, chunk=chunk, per_step=per_step)


def _run_trunk(x, s_hgrn, conv_hist, h_rg, stream_start, w):
    batch, seq, _ = x.shape
    m = batch * seq
    t = _stream_tiles(batch, seq)
    xf = x.reshape(m, D_MODEL)
    new_s, new_conv, new_h = [], [], []
    for layer in range(len(w["w_in"])):
        proj = _in_proj(xf, w["pre_norm_g"][layer], w["w_in"][layer], tm=t["tm_in"], tn=1024)
        yh, s_l = _hgrn(proj, w["lbp"], w["hgrn_norm_g"][layer], s_hgrn[layer], layer=layer,
                        batch=batch, seq=seq, tc=t["tc"], chunk=t["chunk"],
                        heads_per_step=t["per_step"])
        hist = conv_hist[layer].reshape(batch, CONV_W - 1, RG_BLOCKS, LANES).transpose(0, 2, 1, 3)
        h0 = h_rg[layer].reshape(batch, RG_BLOCKS, 1, LANES)
        yr, conv_l, h_l = _rglru(
            proj, hist, h0, w["conv_w"][layer], w["conv_b"][layer], w["w_gate_a"][layer],
            w["b_gate_a"][layer], w["w_gate_x"][layer], w["b_gate_x"][layer], w["rg_lambda"][layer],
            batch=batch, seq=seq, tc=t["tc"], blocks_per_step=t["per_step"],
            stream_start=stream_start)
        xf = _out_stage(yh, yr, proj, xf, w["w_branch_hgrn"][layer], w["w_branch_rglru"][layer],
                        w["w_out"][layer], w["post_norm_g"][layer], tm=t["tm_out"])
        new_s.append(s_l)
        new_conv.append(conv_l.transpose(0, 2, 1, 3).reshape(batch, CONV_W - 1, D_MODEL))
        new_h.append(h_l.reshape(batch, D_MODEL))
    return xf.reshape(batch, seq, D_MODEL), jnp.stack(new_s), jnp.stack(new_conv), jnp.stack(new_h)


def _prep_weights(lb_param, pre_norm_g, w_in, hgrn_norm_g, w_branch_hgrn, conv_w, conv_b, w_gate_a,
                  b_gate_a, w_gate_x, b_gate_x, rg_lambda, w_branch_rglru, w_out, post_norm_g):
    depth = w_in.shape[0]

    def rows(a):
        return a.reshape(depth, 1, D_MODEL)

    def blocks(a):
        return a.reshape(depth, RG_BLOCKS, 1, LANES)

    return dict(
        lbp=lb_param.astype(F32).reshape(depth, HEADS, LANES).transpose(1, 0, 2),
        pre_norm_g=rows(pre_norm_g), post_norm_g=rows(post_norm_g),
        hgrn_norm_g=hgrn_norm_g.reshape(depth, HEADS, 1, LANES),
        w_in=w_in.astype(BF16), w_branch_hgrn=w_branch_hgrn.astype(BF16),
        w_branch_rglru=w_branch_rglru.astype(BF16), w_out=w_out.astype(BF16),
        conv_w=conv_w.reshape(depth, CONV_W, RG_BLOCKS, LANES).transpose(0, 2, 1, 3),
        conv_b=blocks(conv_b), w_gate_a=w_gate_a.astype(BF16), b_gate_a=blocks(b_gate_a),
        w_gate_x=w_gate_x.astype(BF16), b_gate_x=blocks(b_gate_x), rg_lambda=blocks(rg_lambda),
    )


def kernel(x_prompt, x_sample, state_hgrn, state_conv, state_rglru, lb_param, pre_norm_g, w_in,
           hgrn_norm_g, w_branch_hgrn, conv_w, conv_b, w_gate_a, b_gate_a, w_gate_x, b_gate_x,
           rg_lambda, w_branch_rglru, w_out, post_norm_g):
    w = _prep_weights(lb_param, pre_norm_g, w_in, hgrn_norm_g, w_branch_hgrn, conv_w, conv_b,
                      w_gate_a, b_gate_a, w_gate_x, b_gate_x, rg_lambda, w_branch_rglru, w_out,
                      post_norm_g)
    depth = w_in.shape[0]
    nb = x_prompt.shape[0]
    s0 = jnp.zeros((depth, nb, HEADS, HEAD_DIM, HEAD_DIM), F32)
    c0 = jnp.zeros((depth, nb, CONV_W - 1, D_MODEL), F32)
    h0 = jnp.zeros((depth, nb, D_MODEL), F32)
    y_p, s_p, c_p, h_p = _run_trunk(x_prompt, s0, c0, h0, True, w)
    y_s, s_s, c_s, h_s = _run_trunk(x_sample, state_hgrn, state_conv, state_rglru, False, w)
    return (y_p, y_s, s_p, c_p, h_p, s_s, c_s, h_s)
```

```python
import functools

import jax
import jax.numpy as jnp
from jax import lax
from jax.experimental import pallas as pl
from jax.experimental.pallas import tpu as pltpu

F32 = jnp.float32
BF16 = jnp.bfloat16

D_MODEL = 2048
HEADS = 16
HEAD_DIM = 128
RG_BLOCKS = 16
CONV_W = 4
RG_C = 8.0
EPS = 1e-6
LANES = 128
SUBLANES = 8
PROJ = 8 * D_MODEL
N_COL_BLOCKS = PROJ // LANES
CB_Q, CB_F, CB_V, CB_GH, CB_XR, CB_GR, CB_MH, CB_MR = (16 * i for i in range(8))
HGRN_CHUNK = 64
HGRN_UNROLL = 16
LOG2E = 1.4426950408889634
HGRN_HEADS_UNROLL = 8
HGRN_SPAN = 2
HGRN_SAFE_LOG2 = 120.0
SCAN_VREGS = 4
SCAN_PAD = 4
VMEM_LIMIT_BYTES = 56 * 1024 * 1024


def _sigmoid_pair(z):
    e = jnp.exp(-jnp.abs(z))
    inv = 1.0 / (1.0 + e)
    pos = z >= 0
    sig_pos = jnp.where(pos, 1.0, e) * inv
    sig_neg = jnp.where(pos, e, 1.0) * inv
    log_sig = jnp.minimum(z, 0.0) - jnp.log1p(e)
    return sig_pos, sig_neg, log_sig


def _silu(x):
    return x * jax.nn.sigmoid(x)


def _in_proj_kernel(x_ref, g_ref, w_ref, o_ref, u_ref, *, tn):
    @pl.when(pl.program_id(1) == 0)
    def _():
        x = x_ref[...]
        ms = jnp.mean(x * x, axis=-1, keepdims=True)
        u_ref[...] = (x * lax.rsqrt(ms + EPS) * g_ref[...]).astype(BF16)

    res = jnp.dot(u_ref[...], w_ref[...], preferred_element_type=F32)
    for c in range(tn // LANES):
        o_ref[c] = res[:, c * LANES:(c + 1) * LANES]


def _in_proj(x, g, w, *, tm, tn):
    m = x.shape[0]
    return pl.pallas_call(
        functools.partial(_in_proj_kernel, tn=tn),
        grid=(m // tm, PROJ // tn),
        in_specs=[
            pl.BlockSpec((tm, D_MODEL), lambda i, j: (i, 0)),
            pl.BlockSpec((1, D_MODEL), lambda i, j: (0, 0)),
            pl.BlockSpec((D_MODEL, tn), lambda i, j: (0, j)),
        ],
        out_specs=pl.BlockSpec((tn // LANES, tm, LANES), lambda i, j: (j, i, 0)),
        out_shape=jax.ShapeDtypeStruct((N_COL_BLOCKS, m, LANES), F32),
        scratch_shapes=[pltpu.VMEM((tm, D_MODEL), BF16)],
        compiler_params=pltpu.CompilerParams(
            dimension_semantics=("parallel", "arbitrary"), vmem_limit_bytes=VMEM_LIMIT_BYTES),
        name="in_proj",
    )(x, g, w)


def _hgrn_prep(q_raw, z, lb, qkb_ref, shift_ok, *, chunk):
    n_groups = chunk // SUBLANES
    q = _silu(q_raw)
    sig_pos, sig_neg, log_sig = _sigmoid_pair(z)
    if lb is None:
        log_f, k = log_sig, sig_neg
    else:
        log_f = jnp.log(lb + (1.0 - lb) * sig_pos)
        k = (1.0 - lb) * sig_neg
    lf2 = log_f * LOG2E
    b_rows = slice(SUBLANES, SUBLANES + chunk)
    scans = []
    for g in range(n_groups):
        bg = lf2[g * SUBLANES:(g + 1) * SUBLANES]
        for ok, d in zip(shift_ok, (1, 2, 4)):
            bg = bg + jnp.where(ok, pltpu.roll(bg, d, 0), 0.0)
        scans.append(bg)
    qkb_ref[0, b_rows, :] = q
    qkb_ref[1, b_rows, :] = k
    qkb_ref[2, b_rows, :] = jnp.concatenate(scans, axis=0)
    run = None
    for g in range(n_groups):
        lo = SUBLANES + g * SUBLANES
        tot = qkb_ref[2, lo + SUBLANES - 1:lo + SUBLANES, :]
        if run is not None:
            qkb_ref[2, lo:lo + SUBLANES, :] = scans[g] + run
            tot = tot + run
        run = tot
    return run


def _hgrn_scores_exact(qkb_ref, consts, *, chunk):
    diag_c, col_c = consts
    n_groups = chunk // SUBLANES

    def grp(plane, g, shift=0):
        lo = SUBLANES + g * SUBLANES - shift
        return qkb_ref[plane, lo:lo + SUBLANES, :]

    def end_row(g):
        lo = SUBLANES + g * SUBLANES + SUBLANES - 1
        return qkb_ref[2, lo:lo + 1, :]

    p_rows = []
    for g in range(n_groups):
        acc = jnp.zeros((SUBLANES, chunk), F32)
        for d in range(SUBLANES):
            a_d = grp(0, g) * grp(1, g, d)
            if d:
                a_d = a_d * jnp.exp2(grp(2, g) - grp(2, g, d))
            acc = jnp.where(diag_c == g * SUBLANES - d, jnp.sum(a_d, axis=-1, keepdims=True), acc)
        p_rows.append(jnp.where(col_c >= g * SUBLANES, acc, 0.0) if g else acc)

    half = chunk // 2
    while half >= SUBLANES:
        gph = half // SUBLANES
        q_parts, k_parts, q_groups = [], [], []
        for g in range(n_groups):
            first = (g // gph) * gph
            if (g // gph) % 2 == 1:
                q_parts.append(grp(0, g) * jnp.exp2(grp(2, g) - end_row(first - 1)))
                q_groups.append(g)
                k_parts.append(jnp.zeros((SUBLANES, LANES), F32))
            else:
                k_parts.append(grp(1, g) * jnp.exp2(end_row(first + gph - 1) - grp(2, g)))
        ql = jnp.concatenate(q_parts, axis=0).astype(BF16)
        kl = jnp.concatenate(k_parts, axis=0).astype(BF16)
        sc = lax.dot_general(ql, kl, (((1,), (1,)), ((), ())), preferred_element_type=F32)
        for j, g in enumerate(q_groups):
            lo = ((g // gph) - 1) * half
            valid = (col_c >= lo) & (col_c < lo + half)
            p_rows[g] = p_rows[g] + jnp.where(valid, sc[j * SUBLANES:(j + 1) * SUBLANES], 0.0)
        half //= 2
    return jnp.concatenate(p_rows, axis=0)


def _block_diag(blocks):
    n = len(blocks)
    zero = jnp.zeros_like(blocks[0])
    return jnp.concatenate(
        [jnp.concatenate([blk if j == i else zero for j in range(n)], axis=1)
         for i, blk in enumerate(blocks)], axis=0)


def _hgrn_finish(runs):
    chunk = runs[0]["v"].shape[0] // len(runs[0]["qkb"])
    rows = slice(SUBLANES, SUBLANES + chunk)
    staged = []
    for r in runs:
        bs = [ref[2, rows, :] for ref in r["qkb"]]
        ks = [ref[1, rows, :] for ref in r["qkb"]]
        q_ins = [(ref[0, rows, :] * jnp.exp2(b)).astype(BF16) for ref, b in zip(r["qkb"], bs)]
        k_outs = [(k * jnp.exp2(e - b)).astype(BF16) for k, b, e in zip(ks, bs, r["b_ends"])]
        upd = jnp.dot(r["v"].T.astype(BF16), _block_diag(k_outs), preferred_element_type=F32)
        p_mat = r["scores_fn"](q_ins, ks, bs).astype(BF16)
        staged.append((q_ins, upd, p_mat))
    states, finals = [], []
    st = None
    for r, (q_ins, upd, p_mat) in zip(runs, staged):
        st = st if r["st"] is None else r["st"]
        run_states = []
        for c, e in enumerate(r["b_ends"]):
            run_states.append(st)
            st = st * jnp.exp2(e) + upd[:, c * LANES:(c + 1) * LANES]
        states.append(jnp.concatenate(run_states, axis=1).astype(BF16))
        finals.append(st)
    outs = []
    for r, (q_ins, upd, p_mat), s_cat in zip(runs, staged, states):
        o = jnp.dot(p_mat, r["v"].astype(BF16), preferred_element_type=F32)
        o = o + lax.dot_general(_block_diag(q_ins), s_cat, (((1,), (1,)), ((), ())),
                                preferred_element_type=F32)
        o = o * lax.rsqrt(jnp.mean(o * o, axis=-1, keepdims=True) + EPS)
        outs.append(o * r["ng"] * _silu(r["g_h"]))
    return outs, finals


def _lower_bound(lbp, layer):
    if layer == 0:
        return None
    ex = jnp.exp(lbp - jnp.max(lbp, axis=0, keepdims=True))
    p = ex / jnp.sum(ex, axis=0, keepdims=True)
    return jnp.sum(p[1:layer + 1], axis=0, keepdims=True)


def _hgrn_kernel(lbp_ref, ng_ref, q_ref, f_ref, v_ref, g_ref, s0_ref, y_ref, s_out_ref, st_ref, qkb_ref,
                 stb_ref,
                 *, layer, chunk, n_chunks, heads_per_step, unroll):
    tt = pl.program_id(2)

    @pl.when(tt == 0)
    def _():
        for hh in range(heads_per_step):
            st_ref[hh] = s0_ref[0, hh].T

    for u in range(unroll):
        qkb_ref[u, :, 0:SUBLANES, :] = jnp.zeros((3, SUBLANES, LANES), F32)
    sub = lax.broadcasted_iota(jnp.int32, (SUBLANES, LANES), 0)
    shift_ok = [sub >= d for d in (1, 2, 4)]
    across_heads = n_chunks == 1
    span = 1 if across_heads else min(HGRN_SPAN, unroll)
    n_iter = (heads_per_step if across_heads else n_chunks) // unroll

    def body(idx, carry):
        if across_heads:
            heads = [idx * unroll + u for u in range(unroll)]
            base = 0
        else:
            heads = [idx // n_iter] * unroll
            base = (idx % n_iter) * (unroll * chunk)

        def rows(u, n=1):
            start = 0 if across_heads else pl.multiple_of(base + u * chunk, chunk)
            return pl.ds(start, n * chunk)

        b_ends = [_hgrn_prep(q_ref[heads[u], rows(u), :], f_ref[heads[u], rows(u), :],
                             _lower_bound(lbp_ref[heads[u]], layer), qkb_ref.at[u], shift_ok,
                             chunk=chunk) for u in range(unroll)]
        lowest = functools.reduce(jnp.minimum, b_ends)
        safe = jnp.min(lowest, axis=1, keepdims=True)[0, 0] >= -HGRN_SAFE_LOG2

        def run(scores_fn, state_of):
            starts = list(range(0, unroll, span))
            runs = [dict(qkb=[qkb_ref.at[u + c] for c in range(span)], b_ends=b_ends[u:u + span],
                         v=v_ref[heads[u], rows(u, span), :], g_h=g_ref[heads[u], rows(u, span), :],
                         ng=ng_ref[heads[u]], scores_fn=scores_fn(u),
                         st=state_of(u) if (across_heads or u == 0) else None)
                    for u in starts]
            outs, finals = _hgrn_finish(runs)
            for u, y in zip(starts, outs):
                y_ref[heads[u], rows(u, span), :] = y.astype(BF16)
            if across_heads:
                for u, st in zip(starts, finals):
                    st_ref[heads[u]] = st
            else:
                st_ref[heads[0]] = finals[-1]

        n_saved = unroll if across_heads else 1
        entering = [st_ref[heads[u]] for u in range(n_saved)]
        for u in range(n_saved):
            stb_ref[u] = entering[u]
        t_idx = lax.broadcasted_iota(jnp.int32, (span * chunk, span * chunk), 0)
        s_idx = lax.broadcasted_iota(jnp.int32, (span * chunk, span * chunk), 1)
        keep = s_idx <= t_idx
        for c in range(1, span):
            keep = keep & ((s_idx >= c * chunk) | (t_idx < c * chunk))

        def fast_scores(u):
            def fn(q_ins, ks, bs):
                k_rel = jnp.concatenate([(k * jnp.exp2(-b)).astype(BF16) for k, b in zip(ks, bs)],
                                        axis=0)
                sc = lax.dot_general(jnp.concatenate(q_ins, axis=0), k_rel,
                                     (((1,), (1,)), ((), ())), preferred_element_type=F32)
                return jnp.where(keep, sc, 0.0)
            return fn
        run(fast_scores, lambda u: entering[u])

        @pl.when(jnp.logical_not(safe))
        def _():
            sub_c = lax.broadcasted_iota(jnp.int32, (SUBLANES, chunk), 0)
            col_c = lax.broadcasted_iota(jnp.int32, (SUBLANES, chunk), 1)
            consts = (col_c - sub_c, col_c)

            def exact_scores(u):
                return lambda q_ins, ks, bs: _block_diag(
                    [_hgrn_scores_exact(qkb_ref.at[u + c], consts, chunk=chunk) for c in range(span)])
            run(exact_scores, lambda u: stb_ref[u])
        return carry

    lax.fori_loop(0, n_iter if across_heads else heads_per_step * n_iter, body, 0)

    @pl.when(tt == pl.num_programs(2) - 1)
    def _():
        for hh in range(heads_per_step):
            s_out_ref[0, hh] = st_ref[hh].T


def _hgrn(proj, lbp, ng, s0, *, layer, batch, seq, tc, chunk, heads_per_step):
    m = proj.shape[1]
    nt = seq // tc
    hb = heads_per_step
    n_chunks = tc // chunk
    unroll = min(HGRN_UNROLL, n_chunks) if n_chunks > 1 else min(HGRN_HEADS_UNROLL, hb)

    def sec(cb):
        return pl.BlockSpec((hb, tc, LANES), lambda b, h, t: (cb // hb + h, b * nt + t, 0))

    return pl.pallas_call(
        functools.partial(_hgrn_kernel, layer=layer, chunk=chunk, n_chunks=tc // chunk,
                          heads_per_step=hb, unroll=unroll),
        grid=(batch, HEADS // hb, nt),
        in_specs=[
            pl.BlockSpec((hb,) + lbp.shape[1:], lambda b, h, t: (h, 0, 0)),
            pl.BlockSpec((hb, 1, LANES), lambda b, h, t: (h, 0, 0)),
            sec(CB_Q), sec(CB_F), sec(CB_V), sec(CB_GH),
            pl.BlockSpec((1, hb, HEAD_DIM, HEAD_DIM), lambda b, h, t: (b, h, 0, 0)),
        ],
        out_specs=[
            pl.BlockSpec((hb, tc, LANES), lambda b, h, t: (h, b * nt + t, 0)),
            pl.BlockSpec((1, hb, HEAD_DIM, HEAD_DIM), lambda b, h, t: (b, h, 0, 0)),
        ],
        out_shape=[
            jax.ShapeDtypeStruct((HEADS, m, LANES), BF16),
            jax.ShapeDtypeStruct((batch, HEADS, HEAD_DIM, HEAD_DIM), F32),
        ],
        scratch_shapes=[pltpu.VMEM((hb, HEAD_DIM, HEAD_DIM), F32),
                        pltpu.VMEM((unroll, 3, SUBLANES + chunk, LANES), F32),
                        pltpu.VMEM((unroll if n_chunks == 1 else 1, HEAD_DIM, HEAD_DIM), F32)],
        compiler_params=pltpu.CompilerParams(
            dimension_semantics=("parallel", "parallel", "arbitrary"),
            vmem_limit_bytes=VMEM_LIMIT_BYTES),
        name="hgrn2",
    )(lbp, ng, proj, proj, proj, proj, s0)


def _rglru_kernel(xr_ref, gr_ref, hist_ref, h0_ref, cw_ref, cb_ref, wa_ref, ba_ref, wx_ref, bx_ref,
                  lam_ref, y_ref, conv_out_ref, h_out_ref, xbuf_ref, hcar_ref, a_ref, b_ref, p_ref, l_ref,
                  *, tc, blocks_per_step, stream_start, n_vregs):
    tt = pl.program_id(2)
    last = tt == pl.num_programs(2) - 1
    hist_lo = SUBLANES - (CONV_W - 1)
    n_seg = SUBLANES * n_vregs
    seg = tc // n_seg
    pitch = seg + SCAN_PAD
    sub = lax.broadcasted_iota(jnp.int32, (SUBLANES, LANES), 0)
    row = lax.broadcasted_iota(jnp.int32, (tc, LANES), 0)

    @pl.when(tt == 0)
    def _():
        for jj in range(blocks_per_step):
            xbuf_ref[jj, hist_lo:SUBLANES, :] = hist_ref[0, jj]
            hcar_ref[jj] = jnp.broadcast_to(h0_ref[0, jj], (SUBLANES, LANES))

    def block_body(jj, carry):
        x = xr_ref[jj]
        xbuf_ref[jj, SUBLANES:SUBLANES + tc, :] = x
        cw = cw_ref[jj]
        xc = cb_ref[jj] + xbuf_ref[jj, pl.ds(hist_lo, tc), :] * cw[0:1, :]
        for j in range(1, CONV_W):
            xc = xc + xbuf_ref[jj, pl.ds(hist_lo + j, tc), :] * cw[j:j + 1, :]
        tail = x[tc - (CONV_W - 1):tc, :]
        xbuf_ref[jj, hist_lo:SUBLANES, :] = tail

        xcb = xc.astype(BF16)
        r = jax.nn.sigmoid(jnp.dot(xcb, wa_ref[jj], preferred_element_type=F32) + ba_ref[jj])
        gi = jax.nn.sigmoid(jnp.dot(xcb, wx_ref[jj], preferred_element_type=F32) + bx_ref[jj])
        nl = -lam_ref[jj]
        rate = -RG_C * (jnp.maximum(nl, 0.0) + jnp.log1p(jnp.exp(-jnp.abs(nl))))
        log_a = r * rate
        a = jnp.exp2(r * (rate * LOG2E))
        m2 = -jnp.tanh(log_a) * (a * a + 1.0)
        mult = jnp.where(m2 > 0.0, m2 * lax.rsqrt(m2), 0.0)
        if stream_start:
            mult = jnp.where((row == 0) & (tt == 0), 1.0, mult)
        bt = mult * gi * xc
        for s in range(n_seg):
            a_ref[s * pitch:s * pitch + seg, :] = a[s * seg:(s + 1) * seg]
            b_ref[s * pitch:s * pitch + seg, :] = bt[s * seg:(s + 1) * seg]

        def scan_body(i, pl_prev):
            out = []
            for v, (p_prev, l_prev) in enumerate(pl_prev):
                rows = pl.ds(i + v * SUBLANES * pitch, SUBLANES, stride=pitch)
                ai = a_ref[rows, :]
                l_new = ai * l_prev + b_ref[rows, :]
                p_new = ai * p_prev
                p_ref[rows, :] = p_new
                l_ref[rows, :] = l_new
                out.append((p_new, l_new))
            return tuple(out)

        init = tuple((jnp.ones((SUBLANES, LANES), F32), jnp.zeros((SUBLANES, LANES), F32))
                     for _ in range(n_vregs))
        ends = lax.fori_loop(0, seg, scan_body, init, unroll=min(8, seg))
        h_in = hcar_ref[jj]
        pieces = []
        for v, (p_end, l_end) in enumerate(ends):
            for d in (1, 2, 4):
                keep = sub >= d
                l_end = p_end * jnp.where(keep, pltpu.roll(l_end, d, 0), 0.0) + l_end
                p_end = p_end * jnp.where(keep, pltpu.roll(p_end, d, 0), 1.0)
            h_seg_end = p_end * h_in + l_end
            h_seg_start = jnp.where(sub >= 1, pltpu.roll(h_seg_end, 1, 0), h_in)
            h_in = jnp.broadcast_to(h_seg_end[SUBLANES - 1:SUBLANES, :], (SUBLANES, LANES))
            for r in range(SUBLANES):
                lo = (v * SUBLANES + r) * pitch
                pieces.append(p_ref[lo:lo + seg, :] * h_seg_start[r:r + 1, :] + l_ref[lo:lo + seg, :])
        h_last = h_in
        hcar_ref[jj] = h_last
        h = jnp.concatenate(pieces, axis=0)
        y_ref[jj] = (h * _silu(gr_ref[jj])).astype(BF16)

        @pl.when(last)
        def _():
            conv_out_ref[0, jj] = tail
            h_out_ref[0, jj] = h_last[0:1, :]
        return carry

    lax.fori_loop(0, blocks_per_step, block_body, 0)


def _rglru(proj, hist, h0, cw, cb, wa, ba, wx, bx, lam, *, batch, seq, tc, blocks_per_step,
           stream_start):
    m = proj.shape[1]
    nt = seq // tc
    jb = blocks_per_step
    n_vregs = min(SCAN_VREGS, tc // SUBLANES)

    def sec(cb_off):
        return pl.BlockSpec((jb, tc, LANES), lambda b, j, t: (cb_off // jb + j, b * nt + t, 0))

    def per_block(shape):
        return pl.BlockSpec((jb,) + shape, lambda b, j, t: (j,) + (0,) * len(shape))

    return pl.pallas_call(
        functools.partial(_rglru_kernel, tc=tc, blocks_per_step=jb, stream_start=stream_start,
                          n_vregs=n_vregs),
        grid=(batch, RG_BLOCKS // jb, nt),
        in_specs=[
            sec(CB_XR), sec(CB_GR),
            pl.BlockSpec((1, jb, CONV_W - 1, LANES), lambda b, j, t: (b, j, 0, 0)),
            pl.BlockSpec((1, jb, 1, LANES), lambda b, j, t: (b, j, 0, 0)),
            per_block((CONV_W, LANES)), per_block((1, LANES)),
            per_block((LANES, LANES)), per_block((1, LANES)),
            per_block((LANES, LANES)), per_block((1, LANES)),
            per_block((1, LANES)),
        ],
        out_specs=[
            pl.BlockSpec((jb, tc, LANES), lambda b, j, t: (j, b * nt + t, 0)),
            pl.BlockSpec((1, jb, CONV_W - 1, LANES), lambda b, j, t: (b, j, 0, 0)),
            pl.BlockSpec((1, jb, 1, LANES), lambda b, j, t: (b, j, 0, 0)),
        ],
        out_shape=[
            jax.ShapeDtypeStruct((RG_BLOCKS, m, LANES), BF16),
            jax.ShapeDtypeStruct((batch, RG_BLOCKS, CONV_W - 1, LANES), F32),
            jax.ShapeDtypeStruct((batch, RG_BLOCKS, 1, LANES), F32),
        ],
        scratch_shapes=[
            pltpu.VMEM((jb, SUBLANES + tc, LANES), F32),
            pltpu.VMEM((jb, SUBLANES, LANES), F32),
        ] + [pltpu.VMEM((tc + SUBLANES * n_vregs * SCAN_PAD, LANES), F32)] * 4,
        compiler_params=pltpu.CompilerParams(
            dimension_semantics=("parallel", "parallel", "arbitrary"),
            vmem_limit_bytes=VMEM_LIMIT_BYTES),
        name="rglru",
    )(proj, proj, hist, h0, cw, cb, wa, ba, wx, bx, lam)


def _out_kernel(yh_ref, yr_ref, mh_ref, mr_ref, x_ref, wbh_ref, wbr_ref, wo_ref, g_ref, o_ref):
    def gather(ref):
        return jnp.concatenate([ref[c] for c in range(D_MODEL // LANES)], axis=1)

    b_h = jnp.dot(gather(yh_ref), wbh_ref[...], preferred_element_type=F32)
    b_r = jnp.dot(gather(yr_ref), wbr_ref[...], preferred_element_type=F32)
    merged = jax.nn.sigmoid(gather(mh_ref)) * b_h + jax.nn.sigmoid(gather(mr_ref)) * b_r
    out = jnp.dot(merged.astype(BF16), wo_ref[...], preferred_element_type=F32)
    ms = jnp.mean(out * out, axis=-1, keepdims=True)
    o_ref[...] = x_ref[...] + out * lax.rsqrt(ms + EPS) * g_ref[...]


def _out_stage(yh, yr, proj, x, wbh, wbr, wo, g, *, tm):
    m = x.shape[0]
    nb = D_MODEL // LANES

    def resident(shape):
        return pl.BlockSpec(shape, lambda i: (0,) * len(shape), pipeline_mode=pl.Buffered(1))

    return pl.pallas_call(
        _out_kernel,
        grid=(m // tm,),
        in_specs=[
            pl.BlockSpec((nb, tm, LANES), lambda i: (0, i, 0)),
            pl.BlockSpec((nb, tm, LANES), lambda i: (0, i, 0)),
            pl.BlockSpec((nb, tm, LANES), lambda i: (CB_MH // nb, i, 0)),
            pl.BlockSpec((nb, tm, LANES), lambda i: (CB_MR // nb, i, 0)),
            pl.BlockSpec((tm, D_MODEL), lambda i: (i, 0)),
            resident((D_MODEL, D_MODEL)), resident((D_MODEL, D_MODEL)), resident((D_MODEL, D_MODEL)),
            resident((1, D_MODEL)),
        ],
        out_specs=pl.BlockSpec((tm, D_MODEL), lambda i: (i, 0)),
        out_shape=jax.ShapeDtypeStruct((m, D_MODEL), F32),
        compiler_params=pltpu.CompilerParams(
            dimension_semantics=("parallel",), vmem_limit_bytes=VMEM_LIMIT_BYTES),
        name="out_stage",
    )(yh, yr, proj, proj, x, wbh, wbr, wo, g)


def _stream_tiles(batch, seq):
    m = batch * seq
    tm_in = min(m, 1024)
    tm_out = min(m, 256)
    tc = min(seq, 1024)
    chunk = min(seq, HGRN_CHUNK)
    per_step = 1 if seq >= 1024 else HEADS
    return dict(tm_in=tm_in, tm_out=tm_out, tc=tc, chunk=chunk, per_step=per_step)


def _run_trunk(x, s_hgrn, conv_hist, h_rg, stream_start, w):
    batch, seq, _ = x.shape
    m = batch * seq
    t = _stream_tiles(batch, seq)
    xf = x.reshape(m, D_MODEL)
    new_s, new_conv, new_h = [], [], []
    for layer in range(len(w["w_in"])):
        proj = _in_proj(xf, w["pre_norm_g"][layer], w["w_in"][layer], tm=t["tm_in"], tn=1024)
        yh, s_l = _hgrn(proj, w["lbp"], w["hgrn_norm_g"][layer], s_hgrn[layer], layer=layer,
                        batch=batch, seq=seq, tc=t["tc"], chunk=t["chunk"],
                        heads_per_step=t["per_step"])
        hist = conv_hist[layer].reshape(batch, CONV_W - 1, RG_BLOCKS, LANES).transpose(0, 2, 1, 3)
        h0 = h_rg[layer].reshape(batch, RG_BLOCKS, 1, LANES)
        yr, conv_l, h_l = _rglru(
            proj, hist, h0, w["conv_w"][layer], w["conv_b"][layer], w["w_gate_a"][layer],
            w["b_gate_a"][layer], w["w_gate_x"][layer], w["b_gate_x"][layer], w["rg_lambda"][layer],
            batch=batch, seq=seq, tc=t["tc"], blocks_per_step=t["per_step"],
            stream_start=stream_start)
        xf = _out_stage(yh, yr, proj, xf, w["w_branch_hgrn"][layer], w["w_branch_rglru"][layer],
                        w["w_out"][layer], w["post_norm_g"][layer], tm=t["tm_out"])
        new_s.append(s_l)
        new_conv.append(conv_l.transpose(0, 2, 1, 3).reshape(batch, CONV_W - 1, D_MODEL))
        new_h.append(h_l.reshape(batch, D_MODEL))
    return xf.reshape(batch, seq, D_MODEL), jnp.stack(new_s), jnp.stack(new_conv), jnp.stack(new_h)


def _prep_weights(lb_param, pre_norm_g, w_in, hgrn_norm_g, w_branch_hgrn, conv_w, conv_b, w_gate_a,
                  b_gate_a, w_gate_x, b_gate_x, rg_lambda, w_branch_rglru, w_out, post_norm_g):
    depth = w_in.shape[0]

    def rows(a):
        return a.reshape(depth, 1, D_MODEL)

    def blocks(a):
        return a.reshape(depth, RG_BLOCKS, 1, LANES)

    def per_layer(a, f):
        return [f(a[l]) for l in range(depth)]

    to_bf16 = lambda a: a.astype(BF16)
    return dict(
        lbp=lb_param.astype(F32).reshape(depth, HEADS, LANES).transpose(1, 0, 2),
        pre_norm_g=rows(pre_norm_g), post_norm_g=rows(post_norm_g),
        hgrn_norm_g=hgrn_norm_g.reshape(depth, HEADS, 1, LANES),
        w_in=per_layer(w_in, to_bf16), w_branch_hgrn=per_layer(w_branch_hgrn, to_bf16),
        w_branch_rglru=per_layer(w_branch_rglru, to_bf16), w_out=per_layer(w_out, to_bf16),
        conv_w=conv_w.reshape(depth, CONV_W, RG_BLOCKS, LANES).transpose(0, 2, 1, 3),
        conv_b=blocks(conv_b), w_gate_a=per_layer(w_gate_a, to_bf16), b_gate_a=blocks(b_gate_a),
        w_gate_x=per_layer(w_gate_x, to_bf16), b_gate_x=blocks(b_gate_x), rg_lambda=blocks(rg_lambda),
    )


def kernel(x_prompt, x_sample, state_hgrn, state_conv, state_rglru, lb_param, pre_norm_g, w_in,
           hgrn_norm_g, w_branch_hgrn, conv_w, conv_b, w_gate_a, b_gate_a, w_gate_x, b_gate_x,
           rg_lambda, w_branch_rglru, w_out, post_norm_g):
    w = _prep_weights(lb_param, pre_norm_g, w_in, hgrn_norm_g, w_branch_hgrn, conv_w, conv_b,
                      w_gate_a, b_gate_a, w_gate_x, b_gate_x, rg_lambda, w_branch_rglru, w_out,
                      post_norm_g)
    depth = w_in.shape[0]
    nb = x_prompt.shape[0]
    s0 = jnp.zeros((depth, nb, HEADS, HEAD_DIM, HEAD_DIM), F32)
    c0 = jnp.zeros((depth, nb, CONV_W - 1, D_MODEL), F32)
    h0 = jnp.zeros((depth, nb, D_MODEL), F32)
    y_p, s_p, c_p, h_p = _run_trunk(x_prompt, s0, c0, h0, True, w)
    y_s, s_s, c_s, h_s = _run_trunk(x_sample, state_hgrn, state_conv, state_rglru, False, w)
    return (y_p, y_s, s_p, c_p, h_p, s_s, c_s, h_s)
```

```python
import functools

import jax
import jax.numpy as jnp
from jax import lax
from jax.experimental import pallas as pl
from jax.experimental.pallas import tpu as pltpu

F32 = jnp.float32
BF16 = jnp.bfloat16

D_MODEL = 2048
HEADS = 16
HEAD_DIM = 128
RG_BLOCKS = 16
CONV_W = 4
RG_C = 8.0
EPS = 1e-6
LANES = 128
SUBLANES = 8
PROJ = 8 * D_MODEL
N_COL_BLOCKS = PROJ // LANES
CB_Q, CB_F, CB_V, CB_GH, CB_XR, CB_GR, CB_MH, CB_MR = (16 * i for i in range(8))
HGRN_CHUNK = 64
HGRN_UNROLL = 16
LOG2E = 1.4426950408889634
HGRN_HEADS_UNROLL = 8
HGRN_SPAN = 2
HGRN_SAFE_LOG2 = 120.0
SCAN_VREGS = 4
SCAN_PAD = 4
VMEM_LIMIT_BYTES = 56 * 1024 * 1024


def _sigmoid_pair(z):
    e = jnp.exp(-jnp.abs(z))
    inv = 1.0 / (1.0 + e)
    pos = z >= 0
    sig_pos = jnp.where(pos, 1.0, e) * inv
    sig_neg = jnp.where(pos, e, 1.0) * inv
    log_sig = jnp.minimum(z, 0.0) - jnp.log1p(e)
    return sig_pos, sig_neg, log_sig


def _silu(x):
    return x * jax.nn.sigmoid(x)


def _in_proj_kernel(x_ref, g_ref, w_ref, o_ref, u_ref, *, tn):
    @pl.when(pl.program_id(1) == 0)
    def _():
        x = x_ref[...]
        ms = jnp.mean(x * x, axis=-1, keepdims=True)
        u_ref[...] = (x * lax.rsqrt(ms + EPS) * g_ref[...]).astype(BF16)

    res = jnp.dot(u_ref[...], w_ref[...], preferred_element_type=F32)
    for c in range(tn // LANES):
        o_ref[c] = res[:, c * LANES:(c + 1) * LANES]


def _in_proj(x, g, w, *, tm, tn):
    m = x.shape[0]
    return pl.pallas_call(
        functools.partial(_in_proj_kernel, tn=tn),
        grid=(m // tm, PROJ // tn),
        in_specs=[
            pl.BlockSpec((tm, D_MODEL), lambda i, j: (i, 0)),
            pl.BlockSpec((1, D_MODEL), lambda i, j: (0, 0)),
            pl.BlockSpec((D_MODEL, tn), lambda i, j: (0, j)),
        ],
        out_specs=pl.BlockSpec((tn // LANES, tm, LANES), lambda i, j: (j, i, 0)),
        out_shape=jax.ShapeDtypeStruct((N_COL_BLOCKS, m, LANES), F32),
        scratch_shapes=[pltpu.VMEM((tm, D_MODEL), BF16)],
        compiler_params=pltpu.CompilerParams(
            dimension_semantics=("parallel", "arbitrary"), vmem_limit_bytes=VMEM_LIMIT_BYTES),
        name="in_proj",
    )(x, g, w)


def _hgrn_prep(q_raw, z, lb, qkb_ref, shift_ok, *, chunk):
    n_groups = chunk // SUBLANES
    q = _silu(q_raw)
    sig_pos, sig_neg, log_sig = _sigmoid_pair(z)
    if lb is None:
        log_f, k = log_sig, sig_neg
    else:
        log_f = jnp.log(lb + (1.0 - lb) * sig_pos)
        k = (1.0 - lb) * sig_neg
    lf2 = log_f * LOG2E
    b_rows = slice(SUBLANES, SUBLANES + chunk)
    scans = []
    for g in range(n_groups):
        bg = lf2[g * SUBLANES:(g + 1) * SUBLANES]
        for ok, d in zip(shift_ok, (1, 2, 4)):
            bg = bg + jnp.where(ok, pltpu.roll(bg, d, 0), 0.0)
        scans.append(bg)
    qkb_ref[0, b_rows, :] = q
    qkb_ref[1, b_rows, :] = k
    qkb_ref[2, b_rows, :] = jnp.concatenate(scans, axis=0)
    run = None
    for g in range(n_groups):
        lo = SUBLANES + g * SUBLANES
        tot = qkb_ref[2, lo + SUBLANES - 1:lo + SUBLANES, :]
        if run is not None:
            qkb_ref[2, lo:lo + SUBLANES, :] = scans[g] + run
            tot = tot + run
        run = tot
    return run


def _hgrn_scores_exact(qkb_ref, consts, *, chunk):
    diag_c, col_c = consts
    n_groups = chunk // SUBLANES

    def grp(plane, g, shift=0):
        lo = SUBLANES + g * SUBLANES - shift
        return qkb_ref[plane, lo:lo + SUBLANES, :]

    def end_row(g):
        lo = SUBLANES + g * SUBLANES + SUBLANES - 1
        return qkb_ref[2, lo:lo + 1, :]

    p_rows = []
    for g in range(n_groups):
        acc = jnp.zeros((SUBLANES, chunk), F32)
        for d in range(SUBLANES):
            a_d = grp(0, g) * grp(1, g, d)
            if d:
                a_d = a_d * jnp.exp2(grp(2, g) - grp(2, g, d))
            acc = jnp.where(diag_c == g * SUBLANES - d, jnp.sum(a_d, axis=-1, keepdims=True), acc)
        p_rows.append(jnp.where(col_c >= g * SUBLANES, acc, 0.0) if g else acc)

    half = chunk // 2
    while half >= SUBLANES:
        gph = half // SUBLANES
        q_parts, k_parts, q_groups = [], [], []
        for g in range(n_groups):
            first = (g // gph) * gph
            if (g // gph) % 2 == 1:
                q_parts.append(grp(0, g) * jnp.exp2(grp(2, g) - end_row(first - 1)))
                q_groups.append(g)
                k_parts.append(jnp.zeros((SUBLANES, LANES), F32))
            else:
                k_parts.append(grp(1, g) * jnp.exp2(end_row(first + gph - 1) - grp(2, g)))
        ql = jnp.concatenate(q_parts, axis=0).astype(BF16)
        kl = jnp.concatenate(k_parts, axis=0).astype(BF16)
        sc = lax.dot_general(ql, kl, (((1,), (1,)), ((), ())), preferred_element_type=F32)
        for j, g in enumerate(q_groups):
            lo = ((g // gph) - 1) * half
            valid = (col_c >= lo) & (col_c < lo + half)
            p_rows[g] = p_rows[g] + jnp.where(valid, sc[j * SUBLANES:(j + 1) * SUBLANES], 0.0)
        half //= 2
    return jnp.concatenate(p_rows, axis=0)


def _block_diag(blocks):
    n = len(blocks)
    zero = jnp.zeros_like(blocks[0])
    return jnp.concatenate(
        [jnp.concatenate([blk if j == i else zero for j in range(n)], axis=1)
         for i, blk in enumerate(blocks)], axis=0)


def _hgrn_finish(runs):
    chunk = runs[0]["v"].shape[0] // len(runs[0]["qkb"])
    rows = slice(SUBLANES, SUBLANES + chunk)
    staged = []
    for r in runs:
        bs = [ref[2, rows, :] for ref in r["qkb"]]
        ks = [ref[1, rows, :] for ref in r["qkb"]]
        q_ins = [(ref[0, rows, :] * jnp.exp2(b)).astype(BF16) for ref, b in zip(r["qkb"], bs)]
        k_outs = [(k * jnp.exp2(e - b)).astype(BF16) for k, b, e in zip(ks, bs, r["b_ends"])]
        upd = jnp.dot(r["v"].T.astype(BF16), _block_diag(k_outs), preferred_element_type=F32)
        p_mat = r["scores_fn"](q_ins, ks, bs).astype(BF16)
        staged.append((q_ins, upd, p_mat))
    states, finals = [], []
    st = None
    for r, (q_ins, upd, p_mat) in zip(runs, staged):
        st = st if r["st"] is None else r["st"]
        run_states = []
        for c, e in enumerate(r["b_ends"]):
            run_states.append(st)
            st = st * jnp.exp2(e) + upd[:, c * LANES:(c + 1) * LANES]
        states.append(jnp.concatenate(run_states, axis=1).astype(BF16))
        finals.append(st)
    outs = []
    for r, (q_ins, upd, p_mat), s_cat in zip(runs, staged, states):
        o = jnp.dot(p_mat, r["v"].astype(BF16), preferred_element_type=F32)
        o = o + lax.dot_general(_block_diag(q_ins), s_cat, (((1,), (1,)), ((), ())),
                                preferred_element_type=F32)
        o = o * lax.rsqrt(jnp.mean(o * o, axis=-1, keepdims=True) + EPS)
        outs.append(o * r["ng"] * _silu(r["g_h"]))
    return outs, finals


def _lower_bound(lbp, layer):
    if layer == 0:
        return None
    ex = jnp.exp(lbp - jnp.max(lbp, axis=0, keepdims=True))
    p = ex / jnp.sum(ex, axis=0, keepdims=True)
    return jnp.sum(p[1:layer + 1], axis=0, keepdims=True)


def _hgrn_kernel(lbp_ref, ng_ref, q_ref, f_ref, v_ref, g_ref, s0_ref, y_ref, s_out_ref, st_ref, qkb_ref,
                 stb_ref,
                 *, layer, chunk, n_chunks, heads_per_step, unroll):
    tt = pl.program_id(2)

    @pl.when(tt == 0)
    def _():
        for hh in range(heads_per_step):
            st_ref[hh] = s0_ref[0, hh].T

    for u in range(unroll):
        qkb_ref[u, :, 0:SUBLANES, :] = jnp.zeros((3, SUBLANES, LANES), F32)
    sub = lax.broadcasted_iota(jnp.int32, (SUBLANES, LANES), 0)
    shift_ok = [sub >= d for d in (1, 2, 4)]
    across_heads = n_chunks == 1
    span = 1 if across_heads else min(HGRN_SPAN, unroll)
    n_iter = (heads_per_step if across_heads else n_chunks) // unroll

    def body(idx, carry):
        if across_heads:
            heads = [idx * unroll + u for u in range(unroll)]
            base = 0
        else:
            heads = [idx // n_iter] * unroll
            base = (idx % n_iter) * (unroll * chunk)

        def rows(u, n=1):
            start = 0 if across_heads else pl.multiple_of(base + u * chunk, chunk)
            return pl.ds(start, n * chunk)

        b_ends = [_hgrn_prep(q_ref[heads[u], rows(u), :], f_ref[heads[u], rows(u), :],
                             _lower_bound(lbp_ref[heads[u]], layer), qkb_ref.at[u], shift_ok,
                             chunk=chunk) for u in range(unroll)]
        lowest = functools.reduce(jnp.minimum, b_ends)
        safe = jnp.min(lowest, axis=1, keepdims=True)[0, 0] >= -HGRN_SAFE_LOG2

        def run(scores_fn, state_of):
            starts = list(range(0, unroll, span))
            runs = [dict(qkb=[qkb_ref.at[u + c] for c in range(span)], b_ends=b_ends[u:u + span],
                         v=v_ref[heads[u], rows(u, span), :], g_h=g_ref[heads[u], rows(u, span), :],
                         ng=ng_ref[heads[u]], scores_fn=scores_fn(u),
                         st=state_of(u) if (across_heads or u == 0) else None)
                    for u in starts]
            outs, finals = _hgrn_finish(runs)
            for u, y in zip(starts, outs):
                y_ref[heads[u], rows(u, span), :] = y.astype(BF16)
            if across_heads:
                for u, st in zip(starts, finals):
                    st_ref[heads[u]] = st
            else:
                st_ref[heads[0]] = finals[-1]

        n_saved = unroll if across_heads else 1
        entering = [st_ref[heads[u]] for u in range(n_saved)]
        for u in range(n_saved):
            stb_ref[u] = entering[u]
        t_idx = lax.broadcasted_iota(jnp.int32, (span * chunk, span * chunk), 0)
        s_idx = lax.broadcasted_iota(jnp.int32, (span * chunk, span * chunk), 1)
        keep = s_idx <= t_idx
        for c in range(1, span):
            keep = keep & ((s_idx >= c * chunk) | (t_idx < c * chunk))

        def fast_scores(u):
            def fn(q_ins, ks, bs):
                k_rel = jnp.concatenate([(k * jnp.exp2(-b)).astype(BF16) for k, b in zip(ks, bs)],
                                        axis=0)
                sc = lax.dot_general(jnp.concatenate(q_ins, axis=0), k_rel,
                                     (((1,), (1,)), ((), ())), preferred_element_type=F32)
                return jnp.where(keep, sc, 0.0)
            return fn
        run(fast_scores, lambda u: entering[u])

        @pl.when(jnp.logical_not(safe))
        def _():
            sub_c = lax.broadcasted_iota(jnp.int32, (SUBLANES, chunk), 0)
            col_c = lax.broadcasted_iota(jnp.int32, (SUBLANES, chunk), 1)
            consts = (col_c - sub_c, col_c)

            def exact_scores(u):
                return lambda q_ins, ks, bs: _block_diag(
                    [_hgrn_scores_exact(qkb_ref.at[u + c], consts, chunk=chunk) for c in range(span)])
            run(exact_scores, lambda u: stb_ref[u])
        return carry

    lax.fori_loop(0, n_iter if across_heads else heads_per_step * n_iter, body, 0)

    @pl.when(tt == pl.num_programs(2) - 1)
    def _():
        for hh in range(heads_per_step):
            s_out_ref[0, hh] = st_ref[hh].T


def _hgrn(proj, lbp, ng, s0, *, layer, batch, seq, tc, chunk, heads_per_step):
    m = proj.shape[1]
    nt = seq // tc
    hb = heads_per_step
    n_chunks = tc // chunk
    unroll = min(HGRN_UNROLL, n_chunks) if n_chunks > 1 else min(HGRN_HEADS_UNROLL, hb)

    def sec(cb):
        return pl.BlockSpec((hb, tc, LANES), lambda b, h, t: (cb // hb + h, b * nt + t, 0))

    return pl.pallas_call(
        functools.partial(_hgrn_kernel, layer=layer, chunk=chunk, n_chunks=tc // chunk,
                          heads_per_step=hb, unroll=unroll),
        grid=(batch, HEADS // hb, nt),
        in_specs=[
            pl.BlockSpec((hb,) + lbp.shape[1:], lambda b, h, t: (h, 0, 0)),
            pl.BlockSpec((hb, 1, LANES), lambda b, h, t: (h, 0, 0)),
            sec(CB_Q), sec(CB_F), sec(CB_V), sec(CB_GH),
            pl.BlockSpec((1, hb, HEAD_DIM, HEAD_DIM), lambda b, h, t: (b, h, 0, 0)),
        ],
        out_specs=[
            pl.BlockSpec((hb, tc, LANES), lambda b, h, t: (h, b * nt + t, 0)),
            pl.BlockSpec((1, hb, HEAD_DIM, HEAD_DIM), lambda b, h, t: (b, h, 0, 0)),
        ],
        out_shape=[
            jax.ShapeDtypeStruct((HEADS, m, LANES), BF16),
            jax.ShapeDtypeStruct((batch, HEADS, HEAD_DIM, HEAD_DIM), F32),
        ],
        scratch_shapes=[pltpu.VMEM((hb, HEAD_DIM, HEAD_DIM), F32),
                        pltpu.VMEM((unroll, 3, SUBLANES + chunk, LANES), F32),
                        pltpu.VMEM((unroll if n_chunks == 1 else 1, HEAD_DIM, HEAD_DIM), F32)],
        compiler_params=pltpu.CompilerParams(
            dimension_semantics=("parallel", "parallel", "arbitrary"),
            vmem_limit_bytes=VMEM_LIMIT_BYTES),
        name="hgrn2",
    )(lbp, ng, proj, proj, proj, proj, s0)


def _rglru_kernel(xr_ref, gr_ref, hist_ref, h0_ref, cw_ref, cb_ref, wa_ref, ba_ref, wx_ref, bx_ref,
                  lam_ref, y_ref, conv_out_ref, h_out_ref, xbuf_ref, hcar_ref, a_ref, b_ref, p_ref, l_ref,
                  *, tc, blocks_per_step, stream_start, n_vregs):
    tt = pl.program_id(2)
    last = tt == pl.num_programs(2) - 1
    hist_lo = SUBLANES - (CONV_W - 1)
    n_seg = SUBLANES * n_vregs
    seg = tc // n_seg
    pitch = seg + SCAN_PAD
    sub = lax.broadcasted_iota(jnp.int32, (SUBLANES, LANES), 0)
    row = lax.broadcasted_iota(jnp.int32, (tc, LANES), 0)

    @pl.when(tt == 0)
    def _():
        for jj in range(blocks_per_step):
            xbuf_ref[jj, hist_lo:SUBLANES, :] = hist_ref[0, jj]
            hcar_ref[jj] = jnp.broadcast_to(h0_ref[0, jj], (SUBLANES, LANES))

    def block_body(jj, carry):
        x = xr_ref[jj]
        xbuf_ref[jj, SUBLANES:SUBLANES + tc, :] = x
        cw = cw_ref[jj]
        xc = cb_ref[jj] + xbuf_ref[jj, pl.ds(hist_lo, tc), :] * cw[0:1, :]
        for j in range(1, CONV_W):
            xc = xc + xbuf_ref[jj, pl.ds(hist_lo + j, tc), :] * cw[j:j + 1, :]
        tail = x[tc - (CONV_W - 1):tc, :]
        xbuf_ref[jj, hist_lo:SUBLANES, :] = tail

        xcb = xc.astype(BF16)
        r = jax.nn.sigmoid(jnp.dot(xcb, wa_ref[jj], preferred_element_type=F32) + ba_ref[jj])
        gi = jax.nn.sigmoid(jnp.dot(xcb, wx_ref[jj], preferred_element_type=F32) + bx_ref[jj])
        nl = -lam_ref[jj]
        rate = -RG_C * (jnp.maximum(nl, 0.0) + jnp.log1p(jnp.exp(-jnp.abs(nl))))
        log_a = r * rate
        a = jnp.exp2(r * (rate * LOG2E))
        m2 = -jnp.tanh(log_a) * (a * a + 1.0)
        mult = jnp.where(m2 > 0.0, m2 * lax.rsqrt(m2), 0.0)
        if stream_start:
            mult = jnp.where((row == 0) & (tt == 0), 1.0, mult)
        bt = mult * gi * xc
        for s in range(n_seg):
            a_ref[s * pitch:s * pitch + seg, :] = a[s * seg:(s + 1) * seg]
            b_ref[s * pitch:s * pitch + seg, :] = bt[s * seg:(s + 1) * seg]

        def scan_body(i, pl_prev):
            out = []
            for v, (p_prev, l_prev) in enumerate(pl_prev):
                rows = pl.ds(i + v * SUBLANES * pitch, SUBLANES, stride=pitch)
                ai = a_ref[rows, :]
                l_new = ai * l_prev + b_ref[rows, :]
                p_new = ai * p_prev
                p_ref[rows, :] = p_new
                l_ref[rows, :] = l_new
                out.append((p_new, l_new))
            return tuple(out)

        init = tuple((jnp.ones((SUBLANES, LANES), F32), jnp.zeros((SUBLANES, LANES), F32))
                     for _ in range(n_vregs))
        ends = lax.fori_loop(0, seg, scan_body, init, unroll=min(8, seg))
        h_in = hcar_ref[jj]
        pieces = []
        for v, (p_end, l_end) in enumerate(ends):
            for d in (1, 2, 4):
                keep = sub >= d
                l_end = p_end * jnp.where(keep, pltpu.roll(l_end, d, 0), 0.0) + l_end
                p_end = p_end * jnp.where(keep, pltpu.roll(p_end, d, 0), 1.0)
            h_seg_end = p_end * h_in + l_end
            h_seg_start = jnp.where(sub >= 1, pltpu.roll(h_seg_end, 1, 0), h_in)
            h_in = jnp.broadcast_to(h_seg_end[SUBLANES - 1:SUBLANES, :], (SUBLANES, LANES))
            for r in range(SUBLANES):
                lo = (v * SUBLANES + r) * pitch
                pieces.append(p_ref[lo:lo + seg, :] * h_seg_start[r:r + 1, :] + l_ref[lo:lo + seg, :])
        h_last = h_in
        hcar_ref[jj] = h_last
        h = jnp.concatenate(pieces, axis=0)
        y_ref[jj] = (h * _silu(gr_ref[jj])).astype(BF16)

        @pl.when(last)
        def _():
            conv_out_ref[0, jj] = tail
            h_out_ref[0, jj] = h_last[0:1, :]
        return carry

    lax.fori_loop(0, blocks_per_step, block_body, 0)


def _rglru(proj, hist, h0, cw, cb, wa, ba, wx, bx, lam, *, batch, seq, tc, blocks_per_step,
           stream_start):
    m = proj.shape[1]
    nt = seq // tc
    jb = blocks_per_step
    n_vregs = min(SCAN_VREGS, tc // SUBLANES)

    def sec(cb_off):
        return pl.BlockSpec((jb, tc, LANES), lambda b, j, t: (cb_off // jb + j, b * nt + t, 0))

    def per_block(shape):
        return pl.BlockSpec((jb,) + shape, lambda b, j, t: (j,) + (0,) * len(shape))

    return pl.pallas_call(
        functools.partial(_rglru_kernel, tc=tc, blocks_per_step=jb, stream_start=stream_start,
                          n_vregs=n_vregs),
        grid=(batch, RG_BLOCKS // jb, nt),
        in_specs=[
            sec(CB_XR), sec(CB_GR),
            pl.BlockSpec((1, jb, CONV_W - 1, LANES), lambda b, j, t: (b, j, 0, 0)),
            pl.BlockSpec((1, jb, 1, LANES), lambda b, j, t: (b, j, 0, 0)),
            per_block((CONV_W, LANES)), per_block((1, LANES)),
            per_block((LANES, LANES)), per_block((1, LANES)),
            per_block((LANES, LANES)), per_block((1, LANES)),
            per_block((1, LANES)),
        ],
        out_specs=[
            pl.BlockSpec((jb, tc, LANES), lambda b, j, t: (j, b * nt + t, 0)),
            pl.BlockSpec((1, jb, CONV_W - 1, LANES), lambda b, j, t: (b, j, 0, 0)),
            pl.BlockSpec((1, jb, 1, LANES), lambda b, j, t: (b, j, 0, 0)),
        ],
        out_shape=[
            jax.ShapeDtypeStruct((RG_BLOCKS, m, LANES), BF16),
            jax.ShapeDtypeStruct((batch, RG_BLOCKS, CONV_W - 1, LANES), F32),
            jax.ShapeDtypeStruct((batch, RG_BLOCKS, 1, LANES), F32),
        ],
        scratch_shapes=[
            pltpu.VMEM((jb, SUBLANES + tc, LANES), F32),
            pltpu.VMEM((jb, SUBLANES, LANES), F32),
        ] + [pltpu.VMEM((tc + SUBLANES * n_vregs * SCAN_PAD, LANES), F32)] * 4,
        compiler_params=pltpu.CompilerParams(
            dimension_semantics=("parallel", "parallel", "arbitrary"),
            vmem_limit_bytes=VMEM_LIMIT_BYTES),
        name="rglru",
    )(proj, proj, hist, h0, cw, cb, wa, ba, wx, bx, lam)


def _out_kernel(yh_ref, yr_ref, mh_ref, mr_ref, x_ref, wbh_ref, wbr_ref, wo_ref, g_ref, o_ref):
    def gather(ref):
        return jnp.concatenate([ref[c] for c in range(D_MODEL // LANES)], axis=1)

    b_h = jnp.dot(gather(yh_ref), wbh_ref[...], preferred_element_type=F32)
    b_r = jnp.dot(gather(yr_ref), wbr_ref[...], preferred_element_type=F32)
    merged = jax.nn.sigmoid(gather(mh_ref)) * b_h + jax.nn.sigmoid(gather(mr_ref)) * b_r
    out = jnp.dot(merged.astype(BF16), wo_ref[...], preferred_element_type=F32)
    ms = jnp.mean(out * out, axis=-1, keepdims=True)
    o_ref[...] = x_ref[...] + out * lax.rsqrt(ms + EPS) * g_ref[...]


def _out_stage(yh, yr, proj, x, wbh, wbr, wo, g, *, tm):
    m = x.shape[0]
    nb = D_MODEL // LANES

    def resident(shape):
        return pl.BlockSpec(shape, lambda i: (0,) * len(shape), pipeline_mode=pl.Buffered(1))

    return pl.pallas_call(
        _out_kernel,
        grid=(m // tm,),
        in_specs=[
            pl.BlockSpec((nb, tm, LANES), lambda i: (0, i, 0)),
            pl.BlockSpec((nb, tm, LANES), lambda i: (0, i, 0)),
            pl.BlockSpec((nb, tm, LANES), lambda i: (CB_MH // nb, i, 0)),
            pl.BlockSpec((nb, tm, LANES), lambda i: (CB_MR // nb, i, 0)),
            pl.BlockSpec((tm, D_MODEL), lambda i: (i, 0)),
            resident((D_MODEL, D_MODEL)), resident((D_MODEL, D_MODEL)), resident((D_MODEL, D_MODEL)),
            resident((1, D_MODEL)),
        ],
        out_specs=pl.BlockSpec((tm, D_MODEL), lambda i: (i, 0)),
        out_shape=jax.ShapeDtypeStruct((m, D_MODEL), F32),
        compiler_params=pltpu.CompilerParams(
            dimension_semantics=("parallel",), vmem_limit_bytes=VMEM_LIMIT_BYTES),
        name="out_stage",
    )(yh, yr, proj, proj, x, wbh, wbr, wo, g)


def _stream_tiles(batch, seq):
    m = batch * seq
    tm_in = min(m, 1024)
    tn_in = 2048
    tm_out = min(m, 256)
    tc = min(seq, 4096)
    chunk = min(seq, HGRN_CHUNK)
    per_step = 1 if seq >= 1024 else HEADS
    return dict(tm_in=tm_in, tn_in=tn_in, tm_out=tm_out, tc=tc, chunk=chunk, per_step=per_step)


def _run_trunk(x, s_hgrn, conv_hist, h_rg, stream_start, w):
    batch, seq, _ = x.shape
    m = batch * seq
    t = _stream_tiles(batch, seq)
    xf = x.reshape(m, D_MODEL)
    new_s, new_conv, new_h = [], [], []
    for layer in range(len(w["w_in"])):
        proj = _in_proj(xf, w["pre_norm_g"][layer], w["w_in"][layer], tm=t["tm_in"], tn=t["tn_in"])
        yh, s_l = _hgrn(proj, w["lbp"], w["hgrn_norm_g"][layer], s_hgrn[layer], layer=layer,
                        batch=batch, seq=seq, tc=t["tc"], chunk=t["chunk"],
                        heads_per_step=t["per_step"])
        hist = conv_hist[layer].reshape(batch, CONV_W - 1, RG_BLOCKS, LANES).transpose(0, 2, 1, 3)
        h0 = h_rg[layer].reshape(batch, RG_BLOCKS, 1, LANES)
        yr, conv_l, h_l = _rglru(
            proj, hist, h0, w["conv_w"][layer], w["conv_b"][layer], w["w_gate_a"][layer],
            w["b_gate_a"][layer], w["w_gate_x"][layer], w["b_gate_x"][layer], w["rg_lambda"][layer],
            batch=batch, seq=seq, tc=t["tc"], blocks_per_step=t["per_step"],
            stream_start=stream_start)
        xf = _out_stage(yh, yr, proj, xf, w["w_branch_hgrn"][layer], w["w_branch_rglru"][layer],
                        w["w_out"][layer], w["post_norm_g"][layer], tm=t["tm_out"])
        new_s.append(s_l)
        new_conv.append(conv_l.transpose(0, 2, 1, 3).reshape(batch, CONV_W - 1, D_MODEL))
        new_h.append(h_l.reshape(batch, D_MODEL))
    return xf.reshape(batch, seq, D_MODEL), jnp.stack(new_s), jnp.stack(new_conv), jnp.stack(new_h)


def _prep_weights(lb_param, pre_norm_g, w_in, hgrn_norm_g, w_branch_hgrn, conv_w, conv_b, w_gate_a,
                  b_gate_a, w_gate_x, b_gate_x, rg_lambda, w_branch_rglru, w_out, post_norm_g):
    depth = w_in.shape[0]

    def rows(a):
        return a.reshape(depth, 1, D_MODEL)

    def blocks(a):
        return a.reshape(depth, RG_BLOCKS, 1, LANES)

    def per_layer(a, f):
        return [f(a[l]) for l in range(depth)]

    to_bf16 = lambda a: a.astype(BF16)
    return dict(
        lbp=lb_param.astype(F32).reshape(depth, HEADS, LANES).transpose(1, 0, 2),
        pre_norm_g=rows(pre_norm_g), post_norm_g=rows(post_norm_g),
        hgrn_norm_g=hgrn_norm_g.reshape(depth, HEADS, 1, LANES),
        w_in=per_layer(w_in, to_bf16), w_branch_hgrn=per_layer(w_branch_hgrn, to_bf16),
        w_branch_rglru=per_layer(w_branch_rglru, to_bf16), w_out=per_layer(w_out, to_bf16),
        conv_w=conv_w.reshape(depth, CONV_W, RG_BLOCKS, LANES).transpose(0, 2, 1, 3),
        conv_b=blocks(conv_b), w_gate_a=per_layer(w_gate_a, to_bf16), b_gate_a=blocks(b_gate_a),
        w_gate_x=per_layer(w_gate_x, to_bf16), b_gate_x=blocks(b_gate_x), rg_lambda=blocks(rg_lambda),
    )


def kernel(x_prompt, x_sample, state_hgrn, state_conv, state_rglru, lb_param, pre_norm_g, w_in,
           hgrn_norm_g, w_branch_hgrn, conv_w, conv_b, w_gate_a, b_gate_a, w_gate_x, b_gate_x,
           rg_lambda, w_branch_rglru, w_out, post_norm_g):
    w = _prep_weights(lb_param, pre_norm_g, w_in, hgrn_norm_g, w_branch_hgrn, conv_w, conv_b,
                      w_gate_a, b_gate_a, w_gate_x, b_gate_x, rg_lambda, w_branch_rglru, w_out,
                      post_norm_g)
    depth = w_in.shape[0]
    nb = x_prompt.shape[0]
    s0 = jnp.zeros((depth, nb, HEADS, HEAD_DIM, HEAD_DIM), F32)
    c0 = jnp.zeros((depth, nb, CONV_W - 1, D_MODEL), F32)
    h0 = jnp.zeros((depth, nb, D_MODEL), F32)
    y_p, s_p, c_p, h_p = _run_trunk(x_prompt, s0, c0, h0, True, w)
    y_s, s_s, c_s, h_s = _run_trunk(x_sample, state_hgrn, state_conv, state_rglru, False, w)
    return (y_p, y_s, s_p, c_p, h_p, s_s, c_s, h_s)
```

```python
import functools

import jax
import jax.numpy as jnp
from jax import lax
from jax.experimental import pallas as pl
from jax.experimental.pallas import tpu as pltpu

F32 = jnp.float32
BF16 = jnp.bfloat16

D_MODEL = 2048
HEADS = 16
HEAD_DIM = 128
RG_BLOCKS = 16
CONV_W = 4
RG_C = 8.0
EPS = 1e-6
LANES = 128
SUBLANES = 8
PROJ = 8 * D_MODEL
N_COL_BLOCKS = PROJ // LANES
CB_Q, CB_F, CB_V, CB_GH, CB_XR, CB_GR, CB_MH, CB_MR = (16 * i for i in range(8))
HGRN_CHUNK = 64
HGRN_UNROLL = 16
LOG2E = 1.4426950408889634
HGRN_HEADS_UNROLL = 8
HGRN_SPAN = 2
HGRN_SAFE_LOG2 = 120.0
SCAN_VREGS = 4
SCAN_PAD = 4
VMEM_LIMIT_BYTES = 56 * 1024 * 1024


def _sigmoid_pair(z):
    e = jnp.exp(-jnp.abs(z))
    inv = 1.0 / (1.0 + e)
    pos = z >= 0
    sig_pos = jnp.where(pos, 1.0, e) * inv
    sig_neg = jnp.where(pos, e, 1.0) * inv
    log_sig = jnp.minimum(z, 0.0) - jnp.log1p(e)
    return sig_pos, sig_neg, log_sig


def _silu(x):
    return x * jax.nn.sigmoid(x)


def _in_proj_kernel(x_ref, g_ref, w_ref, o_ref, u_ref, *, tn):
    @pl.when(pl.program_id(1) == 0)
    def _():
        x = x_ref[...]
        ms = jnp.mean(x * x, axis=-1, keepdims=True)
        u_ref[...] = (x * lax.rsqrt(ms + EPS) * g_ref[...]).astype(BF16)

    res = jnp.dot(u_ref[...], w_ref[...], preferred_element_type=F32)
    for c in range(tn // LANES):
        o_ref[c] = res[:, c * LANES:(c + 1) * LANES]


def _in_proj(x, g, w, *, layer, tm, tn):
    m = x.shape[0]
    return pl.pallas_call(
        functools.partial(_in_proj_kernel, tn=tn),
        grid=(m // tm, PROJ // tn),
        in_specs=[
            pl.BlockSpec((tm, D_MODEL), lambda i, j: (i, 0)),
            pl.BlockSpec((1, D_MODEL), lambda i, j: (0, 0)),
            pl.BlockSpec((None, D_MODEL, tn), lambda i, j: (layer, 0, j)),
        ],
        out_specs=pl.BlockSpec((tn // LANES, tm, LANES), lambda i, j: (j, i, 0)),
        out_shape=jax.ShapeDtypeStruct((N_COL_BLOCKS, m, LANES), F32),
        scratch_shapes=[pltpu.VMEM((tm, D_MODEL), BF16)],
        compiler_params=pltpu.CompilerParams(
            dimension_semantics=("parallel", "arbitrary"), vmem_limit_bytes=VMEM_LIMIT_BYTES),
        name="in_proj",
    )(x, g, w)


def _hgrn_prep(q_raw, z, lb, qkb_ref, shift_ok, *, chunk):
    n_groups = chunk // SUBLANES
    q = _silu(q_raw)
    sig_pos, sig_neg, log_sig = _sigmoid_pair(z)
    if lb is None:
        log_f, k = log_sig, sig_neg
    else:
        log_f = jnp.log(lb + (1.0 - lb) * sig_pos)
        k = (1.0 - lb) * sig_neg
    lf2 = log_f * LOG2E
    b_rows = slice(SUBLANES, SUBLANES + chunk)
    scans = []
    for g in range(n_groups):
        bg = lf2[g * SUBLANES:(g + 1) * SUBLANES]
        for ok, d in zip(shift_ok, (1, 2, 4)):
            bg = bg + jnp.where(ok, pltpu.roll(bg, d, 0), 0.0)
        scans.append(bg)
    qkb_ref[0, b_rows, :] = q
    qkb_ref[1, b_rows, :] = k
    qkb_ref[2, b_rows, :] = jnp.concatenate(scans, axis=0)
    run = None
    for g in range(n_groups):
        lo = SUBLANES + g * SUBLANES
        tot = qkb_ref[2, lo + SUBLANES - 1:lo + SUBLANES, :]
        if run is not None:
            qkb_ref[2, lo:lo + SUBLANES, :] = scans[g] + run
            tot = tot + run
        run = tot
    return run


def _hgrn_scores_exact(qkb_ref, consts, *, chunk):
    diag_c, col_c = consts
    n_groups = chunk // SUBLANES

    def grp(plane, g, shift=0):
        lo = SUBLANES + g * SUBLANES - shift
        return qkb_ref[plane, lo:lo + SUBLANES, :]

    def end_row(g):
        lo = SUBLANES + g * SUBLANES + SUBLANES - 1
        return qkb_ref[2, lo:lo + 1, :]

    p_rows = []
    for g in range(n_groups):
        acc = jnp.zeros((SUBLANES, chunk), F32)
        for d in range(SUBLANES):
            a_d = grp(0, g) * grp(1, g, d)
            if d:
                a_d = a_d * jnp.exp2(grp(2, g) - grp(2, g, d))
            acc = jnp.where(diag_c == g * SUBLANES - d, jnp.sum(a_d, axis=-1, keepdims=True), acc)
        p_rows.append(jnp.where(col_c >= g * SUBLANES, acc, 0.0) if g else acc)

    half = chunk // 2
    while half >= SUBLANES:
        gph = half // SUBLANES
        q_parts, k_parts, q_groups = [], [], []
        for g in range(n_groups):
            first = (g // gph) * gph
            if (g // gph) % 2 == 1:
                q_parts.append(grp(0, g) * jnp.exp2(grp(2, g) - end_row(first - 1)))
                q_groups.append(g)
                k_parts.append(jnp.zeros((SUBLANES, LANES), F32))
            else:
                k_parts.append(grp(1, g) * jnp.exp2(end_row(first + gph - 1) - grp(2, g)))
        ql = jnp.concatenate(q_parts, axis=0).astype(BF16)
        kl = jnp.concatenate(k_parts, axis=0).astype(BF16)
        sc = lax.dot_general(ql, kl, (((1,), (1,)), ((), ())), preferred_element_type=F32)
        for j, g in enumerate(q_groups):
            lo = ((g // gph) - 1) * half
            valid = (col_c >= lo) & (col_c < lo + half)
            p_rows[g] = p_rows[g] + jnp.where(valid, sc[j * SUBLANES:(j + 1) * SUBLANES], 0.0)
        half //= 2
    return jnp.concatenate(p_rows, axis=0)


def _block_diag(blocks):
    n = len(blocks)
    zero = jnp.zeros_like(blocks[0])
    return jnp.concatenate(
        [jnp.concatenate([blk if j == i else zero for j in range(n)], axis=1)
         for i, blk in enumerate(blocks)], axis=0)


def _hgrn_finish(runs):
    chunk = runs[0]["v"].shape[0] // len(runs[0]["qkb"])
    rows = slice(SUBLANES, SUBLANES + chunk)
    staged = []
    for r in runs:
        bs = [ref[2, rows, :] for ref in r["qkb"]]
        ks = [ref[1, rows, :] for ref in r["qkb"]]
        q_ins = [(ref[0, rows, :] * jnp.exp2(b)).astype(BF16) for ref, b in zip(r["qkb"], bs)]
        k_outs = [(k * jnp.exp2(e - b)).astype(BF16) for k, b, e in zip(ks, bs, r["b_ends"])]
        upd = jnp.dot(r["v"].T.astype(BF16), _block_diag(k_outs), preferred_element_type=F32)
        p_mat = r["scores_fn"](q_ins, ks, bs).astype(BF16)
        staged.append((q_ins, upd, p_mat))
    states, finals = [], []
    st = None
    for r, (q_ins, upd, p_mat) in zip(runs, staged):
        st = st if r["st"] is None else r["st"]
        run_states = []
        for c, e in enumerate(r["b_ends"]):
            run_states.append(st)
            st = st * jnp.exp2(e) + upd[:, c * LANES:(c + 1) * LANES]
        states.append(jnp.concatenate(run_states, axis=1).astype(BF16))
        finals.append(st)
    outs = []
    for r, (q_ins, upd, p_mat), s_cat in zip(runs, staged, states):
        o = jnp.dot(p_mat, r["v"].astype(BF16), preferred_element_type=F32)
        o = o + lax.dot_general(_block_diag(q_ins), s_cat, (((1,), (1,)), ((), ())),
                                preferred_element_type=F32)
        o = o * lax.rsqrt(jnp.mean(o * o, axis=-1, keepdims=True) + EPS)
        outs.append(o * r["ng"] * _silu(r["g_h"]))
    return outs, finals


def _lower_bound(lbp, layer):
    if layer == 0:
        return None
    ex = jnp.exp(lbp - jnp.max(lbp, axis=0, keepdims=True))
    p = ex / jnp.sum(ex, axis=0, keepdims=True)
    return jnp.sum(p[1:layer + 1], axis=0, keepdims=True)


def _hgrn_kernel(lbp_ref, ng_ref, q_ref, f_ref, v_ref, g_ref, s0_ref, y_ref, s_out_ref, st_ref, qkb_ref,
                 stb_ref,
                 *, layer, chunk, n_chunks, heads_per_step, unroll):
    tt = pl.program_id(2)

    @pl.when(tt == 0)
    def _():
        for hh in range(heads_per_step):
            st_ref[hh] = s0_ref[0, hh].T

    for u in range(unroll):
        qkb_ref[u, :, 0:SUBLANES, :] = jnp.zeros((3, SUBLANES, LANES), F32)
    sub = lax.broadcasted_iota(jnp.int32, (SUBLANES, LANES), 0)
    shift_ok = [sub >= d for d in (1, 2, 4)]
    across_heads = n_chunks == 1
    span = 1 if across_heads else min(HGRN_SPAN, unroll)
    n_iter = (heads_per_step if across_heads else n_chunks) // unroll

    def body(idx, carry):
        if across_heads:
            heads = [idx * unroll + u for u in range(unroll)]
            base = 0
        else:
            heads = [idx // n_iter] * unroll
            base = (idx % n_iter) * (unroll * chunk)

        def rows(u, n=1):
            start = 0 if across_heads else pl.multiple_of(base + u * chunk, chunk)
            return pl.ds(start, n * chunk)

        b_ends = [_hgrn_prep(q_ref[heads[u], rows(u), :], f_ref[heads[u], rows(u), :],
                             _lower_bound(lbp_ref[heads[u]], layer), qkb_ref.at[u], shift_ok,
                             chunk=chunk) for u in range(unroll)]
        lowest = functools.reduce(jnp.minimum, b_ends)
        safe = jnp.min(lowest, axis=1, keepdims=True)[0, 0] >= -HGRN_SAFE_LOG2

        def run(scores_fn, state_of):
            starts = list(range(0, unroll, span))
            runs = [dict(qkb=[qkb_ref.at[u + c] for c in range(span)], b_ends=b_ends[u:u + span],
                         v=v_ref[heads[u], rows(u, span), :], g_h=g_ref[heads[u], rows(u, span), :],
                         ng=ng_ref[heads[u]], scores_fn=scores_fn(u),
                         st=state_of(u) if (across_heads or u == 0) else None)
                    for u in starts]
            outs, finals = _hgrn_finish(runs)
            for u, y in zip(starts, outs):
                y_ref[heads[u], rows(u, span), :] = y.astype(BF16)
            if across_heads:
                for u, st in zip(starts, finals):
                    st_ref[heads[u]] = st
            else:
                st_ref[heads[0]] = finals[-1]

        n_saved = unroll if across_heads else 1
        entering = [st_ref[heads[u]] for u in range(n_saved)]
        for u in range(n_saved):
            stb_ref[u] = entering[u]
        t_idx = lax.broadcasted_iota(jnp.int32, (span * chunk, span * chunk), 0)
        s_idx = lax.broadcasted_iota(jnp.int32, (span * chunk, span * chunk), 1)
        keep = s_idx <= t_idx
        for c in range(1, span):
            keep = keep & ((s_idx >= c * chunk) | (t_idx < c * chunk))

        def fast_scores(u):
            def fn(q_ins, ks, bs):
                k_rel = jnp.concatenate([(k * jnp.exp2(-b)).astype(BF16) for k, b in zip(ks, bs)],
                                        axis=0)
                sc = lax.dot_general(jnp.concatenate(q_ins, axis=0), k_rel,
                                     (((1,), (1,)), ((), ())), preferred_element_type=F32)
                return jnp.where(keep, sc, 0.0)
            return fn
        run(fast_scores, lambda u: entering[u])

        @pl.when(jnp.logical_not(safe))
        def _():
            sub_c = lax.broadcasted_iota(jnp.int32, (SUBLANES, chunk), 0)
            col_c = lax.broadcasted_iota(jnp.int32, (SUBLANES, chunk), 1)
            consts = (col_c - sub_c, col_c)

            def exact_scores(u):
                return lambda q_ins, ks, bs: _block_diag(
                    [_hgrn_scores_exact(qkb_ref.at[u + c], consts, chunk=chunk) for c in range(span)])
            run(exact_scores, lambda u: stb_ref[u])
        return carry

    lax.fori_loop(0, n_iter if across_heads else heads_per_step * n_iter, body, 0)

    @pl.when(tt == pl.num_programs(2) - 1)
    def _():
        for hh in range(heads_per_step):
            s_out_ref[0, hh] = st_ref[hh].T


def _hgrn(proj, lbp, ng, s0, *, layer, batch, seq, tc, chunk, heads_per_step):
    m = proj.shape[1]
    nt = seq // tc
    hb = heads_per_step
    n_chunks = tc // chunk
    unroll = min(HGRN_UNROLL, n_chunks) if n_chunks > 1 else min(HGRN_HEADS_UNROLL, hb)

    def sec(cb):
        return pl.BlockSpec((hb, tc, LANES), lambda b, h, t: (cb // hb + h, b * nt + t, 0))

    return pl.pallas_call(
        functools.partial(_hgrn_kernel, layer=layer, chunk=chunk, n_chunks=tc // chunk,
                          heads_per_step=hb, unroll=unroll),
        grid=(batch, HEADS // hb, nt),
        in_specs=[
            pl.BlockSpec((hb,) + lbp.shape[1:], lambda b, h, t: (h, 0, 0)),
            pl.BlockSpec((hb, 1, LANES), lambda b, h, t: (h, 0, 0)),
            sec(CB_Q), sec(CB_F), sec(CB_V), sec(CB_GH),
            pl.BlockSpec((None, 1, hb, HEAD_DIM, HEAD_DIM), lambda b, h, t: (layer, b, h, 0, 0)),
        ],
        out_specs=[
            pl.BlockSpec((hb, tc, LANES), lambda b, h, t: (h, b * nt + t, 0)),
            pl.BlockSpec((1, hb, HEAD_DIM, HEAD_DIM), lambda b, h, t: (b, h, 0, 0)),
        ],
        out_shape=[
            jax.ShapeDtypeStruct((HEADS, m, LANES), BF16),
            jax.ShapeDtypeStruct((batch, HEADS, HEAD_DIM, HEAD_DIM), F32),
        ],
        scratch_shapes=[pltpu.VMEM((hb, HEAD_DIM, HEAD_DIM), F32),
                        pltpu.VMEM((unroll, 3, SUBLANES + chunk, LANES), F32),
                        pltpu.VMEM((unroll if n_chunks == 1 else 1, HEAD_DIM, HEAD_DIM), F32)],
        compiler_params=pltpu.CompilerParams(
            dimension_semantics=("parallel", "parallel", "arbitrary"),
            vmem_limit_bytes=VMEM_LIMIT_BYTES),
        name="hgrn2",
    )(lbp, ng, proj, proj, proj, proj, s0)


def _rglru_kernel(xr_ref, gr_ref, hist_ref, h0_ref, cw_ref, cb_ref, wa_ref, ba_ref, wx_ref, bx_ref,
                  lam_ref, y_ref, conv_out_ref, h_out_ref, xbuf_ref, hcar_ref, a_ref, b_ref, p_ref, l_ref,
                  *, tc, blocks_per_step, stream_start, n_vregs):
    tt = pl.program_id(2)
    last = tt == pl.num_programs(2) - 1
    hist_lo = SUBLANES - (CONV_W - 1)
    n_seg = SUBLANES * n_vregs
    seg = tc // n_seg
    pitch = seg + SCAN_PAD
    sub = lax.broadcasted_iota(jnp.int32, (SUBLANES, LANES), 0)
    row = lax.broadcasted_iota(jnp.int32, (tc, LANES), 0)

    @pl.when(tt == 0)
    def _():
        for jj in range(blocks_per_step):
            xbuf_ref[jj, hist_lo:SUBLANES, :] = hist_ref[0, jj]
            hcar_ref[jj] = jnp.broadcast_to(h0_ref[0, jj], (SUBLANES, LANES))

    def block_body(jj, carry):
        x = xr_ref[jj]
        xbuf_ref[jj, SUBLANES:SUBLANES + tc, :] = x
        cw = cw_ref[jj]
        xc = cb_ref[jj] + xbuf_ref[jj, pl.ds(hist_lo, tc), :] * cw[0:1, :]
        for j in range(1, CONV_W):
            xc = xc + xbuf_ref[jj, pl.ds(hist_lo + j, tc), :] * cw[j:j + 1, :]
        tail = x[tc - (CONV_W - 1):tc, :]
        xbuf_ref[jj, hist_lo:SUBLANES, :] = tail

        xcb = xc.astype(BF16)
        r = jax.nn.sigmoid(jnp.dot(xcb, wa_ref[jj], preferred_element_type=F32) + ba_ref[jj])
        gi = jax.nn.sigmoid(jnp.dot(xcb, wx_ref[jj], preferred_element_type=F32) + bx_ref[jj])
        nl = -lam_ref[jj]
        rate = -RG_C * (jnp.maximum(nl, 0.0) + jnp.log1p(jnp.exp(-jnp.abs(nl))))
        log_a = r * rate
        a = jnp.exp2(r * (rate * LOG2E))
        m2 = -jnp.tanh(log_a) * (a * a + 1.0)
        mult = jnp.where(m2 > 0.0, m2 * lax.rsqrt(m2), 0.0)
        if stream_start:
            mult = jnp.where((row == 0) & (tt == 0), 1.0, mult)
        bt = mult * gi * xc
        for s in range(n_seg):
            a_ref[s * pitch:s * pitch + seg, :] = a[s * seg:(s + 1) * seg]
            b_ref[s * pitch:s * pitch + seg, :] = bt[s * seg:(s + 1) * seg]

        def scan_body(i, pl_prev):
            out = []
            for v, (p_prev, l_prev) in enumerate(pl_prev):
                rows = pl.ds(i + v * SUBLANES * pitch, SUBLANES, stride=pitch)
                ai = a_ref[rows, :]
                l_new = ai * l_prev + b_ref[rows, :]
                p_new = ai * p_prev
                p_ref[rows, :] = p_new
                l_ref[rows, :] = l_new
                out.append((p_new, l_new))
            return tuple(out)

        init = tuple((jnp.ones((SUBLANES, LANES), F32), jnp.zeros((SUBLANES, LANES), F32))
                     for _ in range(n_vregs))
        ends = lax.fori_loop(0, seg, scan_body, init, unroll=min(8, seg))
        h_in = hcar_ref[jj]
        pieces = []
        for v, (p_end, l_end) in enumerate(ends):
            for d in (1, 2, 4):
                keep = sub >= d
                l_end = p_end * jnp.where(keep, pltpu.roll(l_end, d, 0), 0.0) + l_end
                p_end = p_end * jnp.where(keep, pltpu.roll(p_end, d, 0), 1.0)
            h_seg_end = p_end * h_in + l_end
            h_seg_start = jnp.where(sub >= 1, pltpu.roll(h_seg_end, 1, 0), h_in)
            h_in = jnp.broadcast_to(h_seg_end[SUBLANES - 1:SUBLANES, :], (SUBLANES, LANES))
            for r in range(SUBLANES):
                lo = (v * SUBLANES + r) * pitch
                pieces.append(p_ref[lo:lo + seg, :] * h_seg_start[r:r + 1, :] + l_ref[lo:lo + seg, :])
        h_last = h_in
        hcar_ref[jj] = h_last
        h = jnp.concatenate(pieces, axis=0)
        y_ref[jj] = (h * _silu(gr_ref[jj])).astype(BF16)

        @pl.when(last)
        def _():
            conv_out_ref[0, jj] = tail
            h_out_ref[0, jj] = h_last[0:1, :]
        return carry

    lax.fori_loop(0, blocks_per_step, block_body, 0)


def _rglru(proj, hist, h0, cw, cb, wa, ba, wx, bx, lam, *, batch, seq, tc, blocks_per_step,
           stream_start):
    m = proj.shape[1]
    nt = seq // tc
    jb = blocks_per_step
    n_vregs = min(SCAN_VREGS, tc // SUBLANES)

    def sec(cb_off):
        return pl.BlockSpec((jb, tc, LANES), lambda b, j, t: (cb_off // jb + j, b * nt + t, 0))

    def per_block(shape):
        return pl.BlockSpec((jb,) + shape, lambda b, j, t: (j,) + (0,) * len(shape))

    return pl.pallas_call(
        functools.partial(_rglru_kernel, tc=tc, blocks_per_step=jb, stream_start=stream_start,
                          n_vregs=n_vregs),
        grid=(batch, RG_BLOCKS // jb, nt),
        in_specs=[
            sec(CB_XR), sec(CB_GR),
            pl.BlockSpec((1, jb, CONV_W - 1, LANES), lambda b, j, t: (b, j, 0, 0)),
            pl.BlockSpec((1, jb, 1, LANES), lambda b, j, t: (b, j, 0, 0)),
            per_block((CONV_W, LANES)), per_block((1, LANES)),
            per_block((LANES, LANES)), per_block((1, LANES)),
            per_block((LANES, LANES)), per_block((1, LANES)),
            per_block((1, LANES)),
        ],
        out_specs=[
            pl.BlockSpec((jb, tc, LANES), lambda b, j, t: (j, b * nt + t, 0)),
            pl.BlockSpec((1, jb, CONV_W - 1, LANES), lambda b, j, t: (b, j, 0, 0)),
            pl.BlockSpec((1, jb, 1, LANES), lambda b, j, t: (b, j, 0, 0)),
        ],
        out_shape=[
            jax.ShapeDtypeStruct((RG_BLOCKS, m, LANES), BF16),
            jax.ShapeDtypeStruct((batch, RG_BLOCKS, CONV_W - 1, LANES), F32),
            jax.ShapeDtypeStruct((batch, RG_BLOCKS, 1, LANES), F32),
        ],
        scratch_shapes=[
            pltpu.VMEM((jb, SUBLANES + tc, LANES), F32),
            pltpu.VMEM((jb, SUBLANES, LANES), F32),
        ] + [pltpu.VMEM((tc + SUBLANES * n_vregs * SCAN_PAD, LANES), F32)] * 4,
        compiler_params=pltpu.CompilerParams(
            dimension_semantics=("parallel", "parallel", "arbitrary"),
            vmem_limit_bytes=VMEM_LIMIT_BYTES),
        name="rglru",
    )(proj, proj, hist, h0, cw, cb, wa, ba, wx, bx, lam)


def _out_kernel(yh_ref, yr_ref, mh_ref, mr_ref, x_ref, wbh_ref, wbr_ref, wo_ref, g_ref, o_ref):
    def gather(ref):
        return jnp.concatenate([ref[c] for c in range(D_MODEL // LANES)], axis=1)

    b_h = jnp.dot(gather(yh_ref), wbh_ref[...], preferred_element_type=F32)
    b_r = jnp.dot(gather(yr_ref), wbr_ref[...], preferred_element_type=F32)
    merged = jax.nn.sigmoid(gather(mh_ref)) * b_h + jax.nn.sigmoid(gather(mr_ref)) * b_r
    out = jnp.dot(merged.astype(BF16), wo_ref[...], preferred_element_type=F32)
    ms = jnp.mean(out * out, axis=-1, keepdims=True)
    o_ref[...] = x_ref[...] + out * lax.rsqrt(ms + EPS) * g_ref[...]


def _out_stage(yh, yr, proj, x, wbh, wbr, wo, g, *, layer, tm):
    m = x.shape[0]
    nb = D_MODEL // LANES

    def resident(shape):
        return pl.BlockSpec(shape, lambda i: (0,) * len(shape), pipeline_mode=pl.Buffered(1))

    stacked_weight = pl.BlockSpec((None, D_MODEL, D_MODEL), lambda i: (layer, 0, 0),
                                  pipeline_mode=pl.Buffered(1))

    return pl.pallas_call(
        _out_kernel,
        grid=(m // tm,),
        in_specs=[
            pl.BlockSpec((nb, tm, LANES), lambda i: (0, i, 0)),
            pl.BlockSpec((nb, tm, LANES), lambda i: (0, i, 0)),
            pl.BlockSpec((nb, tm, LANES), lambda i: (CB_MH // nb, i, 0)),
            pl.BlockSpec((nb, tm, LANES), lambda i: (CB_MR // nb, i, 0)),
            pl.BlockSpec((tm, D_MODEL), lambda i: (i, 0)),
            stacked_weight, stacked_weight, stacked_weight,
            resident((1, D_MODEL)),
        ],
        out_specs=pl.BlockSpec((tm, D_MODEL), lambda i: (i, 0)),
        out_shape=jax.ShapeDtypeStruct((m, D_MODEL), F32),
        compiler_params=pltpu.CompilerParams(
            dimension_semantics=("parallel",), vmem_limit_bytes=VMEM_LIMIT_BYTES),
        name="out_stage",
    )(yh, yr, proj, proj, x, wbh, wbr, wo, g)


def _stream_tiles(batch, seq):
    m = batch * seq
    tm_in = min(m, 1024)
    tn_in = 2048
    tm_out = min(m, 256)
    tc = min(seq, 4096)
    chunk = min(seq, HGRN_CHUNK)
    per_step = 1 if seq >= 1024 else HEADS
    return dict(tm_in=tm_in, tn_in=tn_in, tm_out=tm_out, tc=tc, chunk=chunk, per_step=per_step)


def _run_trunk(x, s_hgrn, conv_hist, h_rg, stream_start, w):
    batch, seq, _ = x.shape
    m = batch * seq
    t = _stream_tiles(batch, seq)
    xf = x.reshape(m, D_MODEL)
    new_s, new_conv, new_h = [], [], []
    for layer in range(w["w_in"].shape[0]):
        proj = _in_proj(xf, w["pre_norm_g"][layer], w["w_in"], layer=layer, tm=t["tm_in"], tn=t["tn_in"])
        yh, s_l = _hgrn(proj, w["lbp"], w["hgrn_norm_g"][layer], s_hgrn, layer=layer,
                        batch=batch, seq=seq, tc=t["tc"], chunk=t["chunk"],
                        heads_per_step=t["per_step"])
        hist = conv_hist[layer].reshape(batch, CONV_W - 1, RG_BLOCKS, LANES).transpose(0, 2, 1, 3)
        h0 = h_rg[layer].reshape(batch, RG_BLOCKS, 1, LANES)
        yr, conv_l, h_l = _rglru(
            proj, hist, h0, w["conv_w"][layer], w["conv_b"][layer], w["w_gate_a"][layer],
            w["b_gate_a"][layer], w["w_gate_x"][layer], w["b_gate_x"][layer], w["rg_lambda"][layer],
            batch=batch, seq=seq, tc=t["tc"], blocks_per_step=t["per_step"],
            stream_start=stream_start)
        xf = _out_stage(yh, yr, proj, xf, w["w_branch_hgrn"], w["w_branch_rglru"], w["w_out"],
                        w["post_norm_g"][layer], layer=layer, tm=t["tm_out"])
        new_s.append(s_l)
        new_conv.append(conv_l.transpose(0, 2, 1, 3).reshape(batch, CONV_W - 1, D_MODEL))
        new_h.append(h_l.reshape(batch, D_MODEL))
    return xf.reshape(batch, seq, D_MODEL), jnp.stack(new_s), jnp.stack(new_conv), jnp.stack(new_h)


def _prep_weights(lb_param, pre_norm_g, w_in, hgrn_norm_g, w_branch_hgrn, conv_w, conv_b, w_gate_a,
                  b_gate_a, w_gate_x, b_gate_x, rg_lambda, w_branch_rglru, w_out, post_norm_g):
    depth = w_in.shape[0]

    def rows(a):
        return a.reshape(depth, 1, D_MODEL)

    def blocks(a):
        return a.reshape(depth, RG_BLOCKS, 1, LANES)

    return dict(
        lbp=lb_param.astype(F32).reshape(depth, HEADS, LANES).transpose(1, 0, 2),
        pre_norm_g=rows(pre_norm_g), post_norm_g=rows(post_norm_g),
        hgrn_norm_g=hgrn_norm_g.reshape(depth, HEADS, 1, LANES),
        w_in=w_in.astype(BF16), w_branch_hgrn=w_branch_hgrn.astype(BF16),
        w_branch_rglru=w_branch_rglru.astype(BF16), w_out=w_out.astype(BF16),
        conv_w=conv_w.reshape(depth, CONV_W, RG_BLOCKS, LANES).transpose(0, 2, 1, 3),
        conv_b=blocks(conv_b), w_gate_a=w_gate_a.astype(BF16), b_gate_a=blocks(b_gate_a),
        w_gate_x=w_gate_x.astype(BF16), b_gate_x=blocks(b_gate_x), rg_lambda=blocks(rg_lambda),
    )


def kernel(x_prompt, x_sample, state_hgrn, state_conv, state_rglru, lb_param, pre_norm_g, w_in,
           hgrn_norm_g, w_branch_hgrn, conv_w, conv_b, w_gate_a, b_gate_a, w_gate_x, b_gate_x,
           rg_lambda, w_branch_rglru, w_out, post_norm_g):
    w = _prep_weights(lb_param, pre_norm_g, w_in, hgrn_norm_g, w_branch_hgrn, conv_w, conv_b,
                      w_gate_a, b_gate_a, w_gate_x, b_gate_x, rg_lambda, w_branch_rglru, w_out,
                      post_norm_g)
    depth = w_in.shape[0]
    nb = x_prompt.shape[0]
    s0 = jnp.zeros((depth, nb, HEADS, HEAD_DIM, HEAD_DIM), F32)
    c0 = jnp.zeros((depth, nb, CONV_W - 1, D_MODEL), F32)
    h0 = jnp.zeros((depth, nb, D_MODEL), F32)
    y_p, s_p, c_p, h_p = _run_trunk(x_prompt, s0, c0, h0, True, w)
    y_s, s_s, c_s, h_s = _run_trunk(x_sample, state_hgrn, state_conv, state_rglru, False, w)
    return (y_p, y_s, s_p, c_p, h_p, s_s, c_s, h_s)
```

```python
import functools

import jax
import jax.numpy as jnp
from jax import lax
from jax.experimental import pallas as pl
from jax.experimental.pallas import tpu as pltpu

F32 = jnp.float32
BF16 = jnp.bfloat16

D_MODEL = 2048
HEADS = 16
HEAD_DIM = 128
RG_BLOCKS = 16
CONV_W = 4
RG_C = 8.0
EPS = 1e-6
LANES = 128
SUBLANES = 8
PROJ = 8 * D_MODEL
N_COL_BLOCKS = PROJ // LANES
CB_Q, CB_F, CB_V, CB_GH, CB_XR, CB_GR, CB_MH, CB_MR = (16 * i for i in range(8))
HGRN_CHUNK = 64
HGRN_UNROLL = 32
LOG2E = 1.4426950408889634
HGRN_HEADS_UNROLL = 16
HGRN_SPAN = 2
HGRN_SAFE_LOG2 = 120.0
SCAN_VREGS = 4
SCAN_PAD = 4
VMEM_LIMIT_BYTES = 56 * 1024 * 1024


def _sigmoid_pair(z):
    e = jnp.exp(-jnp.abs(z))
    inv = 1.0 / (1.0 + e)
    pos = z >= 0
    sig_pos = jnp.where(pos, 1.0, e) * inv
    sig_neg = jnp.where(pos, e, 1.0) * inv
    log_sig = jnp.minimum(z, 0.0) - jnp.log1p(e)
    return sig_pos, sig_neg, log_sig


def _silu(x):
    return x * jax.nn.sigmoid(x)


def _in_proj_kernel(x_ref, g_ref, w_ref, o_ref, u_ref, *, tn):
    @pl.when(pl.program_id(1) == 0)
    def _():
        x = x_ref[...]
        ms = jnp.mean(x * x, axis=-1, keepdims=True)
        u_ref[...] = (x * lax.rsqrt(ms + EPS) * g_ref[...]).astype(BF16)

    res = jnp.dot(u_ref[...], w_ref[...], preferred_element_type=F32)
    for c in range(tn // LANES):
        o_ref[c] = res[:, c * LANES:(c + 1) * LANES]


def _in_proj(x, g, w, *, layer, tm, tn):
    m = x.shape[0]
    return pl.pallas_call(
        functools.partial(_in_proj_kernel, tn=tn),
        grid=(m // tm, PROJ // tn),
        in_specs=[
            pl.BlockSpec((tm, D_MODEL), lambda i, j: (i, 0)),
            pl.BlockSpec((1, D_MODEL), lambda i, j: (0, 0)),
            pl.BlockSpec((None, D_MODEL, tn), lambda i, j: (layer, 0, j)),
        ],
        out_specs=pl.BlockSpec((tn // LANES, tm, LANES), lambda i, j: (j, i, 0)),
        out_shape=jax.ShapeDtypeStruct((N_COL_BLOCKS, m, LANES), F32),
        scratch_shapes=[pltpu.VMEM((tm, D_MODEL), BF16)],
        compiler_params=pltpu.CompilerParams(
            dimension_semantics=("parallel", "arbitrary"), vmem_limit_bytes=VMEM_LIMIT_BYTES),
        name="in_proj",
    )(x, g, w)


def _hgrn_prep(q_raw, z, lb, qkb_ref, shift_ok, *, chunk):
    n_groups = chunk // SUBLANES
    q = _silu(q_raw)
    sig_pos, sig_neg, log_sig = _sigmoid_pair(z)
    if lb is None:
        log_f, k = log_sig, sig_neg
    else:
        log_f = jnp.log(lb + (1.0 - lb) * sig_pos)
        k = (1.0 - lb) * sig_neg
    lf2 = log_f * LOG2E
    b_rows = slice(SUBLANES, SUBLANES + chunk)
    scans = []
    for g in range(n_groups):
        bg = lf2[g * SUBLANES:(g + 1) * SUBLANES]
        for ok, d in zip(shift_ok, (1, 2, 4)):
            bg = bg + jnp.where(ok, pltpu.roll(bg, d, 0), 0.0)
        scans.append(bg)
    qkb_ref[0, b_rows, :] = q
    qkb_ref[1, b_rows, :] = k
    qkb_ref[2, b_rows, :] = jnp.concatenate(scans, axis=0)
    run = None
    for g in range(n_groups):
        lo = SUBLANES + g * SUBLANES
        tot = qkb_ref[2, lo + SUBLANES - 1:lo + SUBLANES, :]
        if run is not None:
            qkb_ref[2, lo:lo + SUBLANES, :] = scans[g] + run
            tot = tot + run
        run = tot
    return run


def _hgrn_scores_exact(qkb_ref, consts, *, chunk):
    diag_c, col_c = consts
    n_groups = chunk // SUBLANES

    def grp(plane, g, shift=0):
        lo = SUBLANES + g * SUBLANES - shift
        return qkb_ref[plane, lo:lo + SUBLANES, :]

    def end_row(g):
        lo = SUBLANES + g * SUBLANES + SUBLANES - 1
        return qkb_ref[2, lo:lo + 1, :]

    p_rows = []
    for g in range(n_groups):
        acc = jnp.zeros((SUBLANES, chunk), F32)
        for d in range(SUBLANES):
            a_d = grp(0, g) * grp(1, g, d)
            if d:
                a_d = a_d * jnp.exp2(grp(2, g) - grp(2, g, d))
            acc = jnp.where(diag_c == g * SUBLANES - d, jnp.sum(a_d, axis=-1, keepdims=True), acc)
        p_rows.append(jnp.where(col_c >= g * SUBLANES, acc, 0.0) if g else acc)

    half = chunk // 2
    while half >= SUBLANES:
        gph = half // SUBLANES
        q_parts, k_parts, q_groups = [], [], []
        for g in range(n_groups):
            first = (g // gph) * gph
            if (g // gph) % 2 == 1:
                q_parts.append(grp(0, g) * jnp.exp2(grp(2, g) - end_row(first - 1)))
                q_groups.append(g)
                k_parts.append(jnp.zeros((SUBLANES, LANES), F32))
            else:
                k_parts.append(grp(1, g) * jnp.exp2(end_row(first + gph - 1) - grp(2, g)))
        ql = jnp.concatenate(q_parts, axis=0).astype(BF16)
        kl = jnp.concatenate(k_parts, axis=0).astype(BF16)
        sc = lax.dot_general(ql, kl, (((1,), (1,)), ((), ())), preferred_element_type=F32)
        for j, g in enumerate(q_groups):
            lo = ((g // gph) - 1) * half
            valid = (col_c >= lo) & (col_c < lo + half)
            p_rows[g] = p_rows[g] + jnp.where(valid, sc[j * SUBLANES:(j + 1) * SUBLANES], 0.0)
        half //= 2
    return jnp.concatenate(p_rows, axis=0)


def _block_diag(blocks):
    n = len(blocks)
    zero = jnp.zeros_like(blocks[0])
    return jnp.concatenate(
        [jnp.concatenate([blk if j == i else zero for j in range(n)], axis=1)
         for i, blk in enumerate(blocks)], axis=0)


def _hgrn_finish(runs):
    chunk = runs[0]["v"].shape[0] // len(runs[0]["qkb"])
    rows = slice(SUBLANES, SUBLANES + chunk)
    staged = []
    for r in runs:
        bs = [ref[2, rows, :] for ref in r["qkb"]]
        ks = [ref[1, rows, :] for ref in r["qkb"]]
        q_ins = [(ref[0, rows, :] * jnp.exp2(b)).astype(BF16) for ref, b in zip(r["qkb"], bs)]
        k_outs = [(k * jnp.exp2(e - b)).astype(BF16) for k, b, e in zip(ks, bs, r["b_ends"])]
        upd = jnp.dot(r["v"].T.astype(BF16), _block_diag(k_outs), preferred_element_type=F32)
        p_mat = r["scores_fn"](q_ins, ks, bs).astype(BF16)
        staged.append((q_ins, upd, p_mat))
    states, finals = [], []
    st = None
    for r, (q_ins, upd, p_mat) in zip(runs, staged):
        st = st if r["st"] is None else r["st"]
        run_states = []
        for c, e in enumerate(r["b_ends"]):
            run_states.append(st)
            st = st * jnp.exp2(e) + upd[:, c * LANES:(c + 1) * LANES]
        states.append(jnp.concatenate(run_states, axis=1).astype(BF16))
        finals.append(st)
    outs = []
    for r, (q_ins, upd, p_mat), s_cat in zip(runs, staged, states):
        o = jnp.dot(p_mat, r["v"].astype(BF16), preferred_element_type=F32)
        o = o + lax.dot_general(_block_diag(q_ins), s_cat, (((1,), (1,)), ((), ())),
                                preferred_element_type=F32)
        o = o * lax.rsqrt(jnp.mean(o * o, axis=-1, keepdims=True) + EPS)
        outs.append(o * r["ng"] * _silu(r["g_h"]))
    return outs, finals


def _lower_bound(lbp, layer):
    if layer == 0:
        return None
    ex = jnp.exp(lbp - jnp.max(lbp, axis=0, keepdims=True))
    p = ex / jnp.sum(ex, axis=0, keepdims=True)
    return jnp.sum(p[1:layer + 1], axis=0, keepdims=True)


def _hgrn_kernel(lbp_ref, ng_ref, q_ref, f_ref, v_ref, g_ref, s0_ref, y_ref, s_out_ref, st_ref, qkb_ref,
                 stb_ref,
                 *, layer, chunk, n_chunks, heads_per_step, unroll):
    tt = pl.program_id(2)

    @pl.when(tt == 0)
    def _():
        for hh in range(heads_per_step):
            st_ref[hh] = s0_ref[0, hh].T

    for u in range(unroll):
        qkb_ref[u, :, 0:SUBLANES, :] = jnp.zeros((3, SUBLANES, LANES), F32)
    sub = lax.broadcasted_iota(jnp.int32, (SUBLANES, LANES), 0)
    shift_ok = [sub >= d for d in (1, 2, 4)]
    across_heads = n_chunks == 1
    span = 1 if across_heads else min(HGRN_SPAN, unroll)
    n_iter = (heads_per_step if across_heads else n_chunks) // unroll

    def body(idx, carry):
        if across_heads:
            heads = [idx * unroll + u for u in range(unroll)]
            base = 0
        else:
            heads = [idx // n_iter] * unroll
            base = (idx % n_iter) * (unroll * chunk)

        def rows(u, n=1):
            start = 0 if across_heads else pl.multiple_of(base + u * chunk, chunk)
            return pl.ds(start, n * chunk)

        b_ends = [_hgrn_prep(q_ref[heads[u], rows(u), :], f_ref[heads[u], rows(u), :],
                             _lower_bound(lbp_ref[heads[u]], layer), qkb_ref.at[u], shift_ok,
                             chunk=chunk) for u in range(unroll)]
        lowest = functools.reduce(jnp.minimum, b_ends)
        safe = jnp.min(lowest, axis=1, keepdims=True)[0, 0] >= -HGRN_SAFE_LOG2

        def run(scores_fn, state_of):
            starts = list(range(0, unroll, span))
            runs = [dict(qkb=[qkb_ref.at[u + c] for c in range(span)], b_ends=b_ends[u:u + span],
                         v=v_ref[heads[u], rows(u, span), :], g_h=g_ref[heads[u], rows(u, span), :],
                         ng=ng_ref[heads[u]], scores_fn=scores_fn(u),
                         st=state_of(u) if (across_heads or u == 0) else None)
                    for u in starts]
            outs, finals = _hgrn_finish(runs)
            for u, y in zip(starts, outs):
                y_ref[heads[u], rows(u, span), :] = y.astype(BF16)
            if across_heads:
                for u, st in zip(starts, finals):
                    st_ref[heads[u]] = st
            else:
                st_ref[heads[0]] = finals[-1]

        n_saved = unroll if across_heads else 1
        entering = [st_ref[heads[u]] for u in range(n_saved)]
        for u in range(n_saved):
            stb_ref[u] = entering[u]
        t_idx = lax.broadcasted_iota(jnp.int32, (span * chunk, span * chunk), 0)
        s_idx = lax.broadcasted_iota(jnp.int32, (span * chunk, span * chunk), 1)
        keep = s_idx <= t_idx
        for c in range(1, span):
            keep = keep & ((s_idx >= c * chunk) | (t_idx < c * chunk))

        def fast_scores(u):
            def fn(q_ins, ks, bs):
                k_rel = jnp.concatenate([(k * jnp.exp2(-b)).astype(BF16) for k, b in zip(ks, bs)],
                                        axis=0)
                sc = lax.dot_general(jnp.concatenate(q_ins, axis=0), k_rel,
                                     (((1,), (1,)), ((), ())), preferred_element_type=F32)
                return jnp.where(keep, sc, 0.0)
            return fn
        run(fast_scores, lambda u: entering[u])

        @pl.when(jnp.logical_not(safe))
        def _():
            sub_c = lax.broadcasted_iota(jnp.int32, (SUBLANES, chunk), 0)
            col_c = lax.broadcasted_iota(jnp.int32, (SUBLANES, chunk), 1)
            consts = (col_c - sub_c, col_c)

            def exact_scores(u):
                return lambda q_ins, ks, bs: _block_diag(
                    [_hgrn_scores_exact(qkb_ref.at[u + c], consts, chunk=chunk) for c in range(span)])
            run(exact_scores, lambda u: stb_ref[u])
        return carry

    lax.fori_loop(0, n_iter if across_heads else heads_per_step * n_iter, body, 0)

    @pl.when(tt == pl.num_programs(2) - 1)
    def _():
        for hh in range(heads_per_step):
            s_out_ref[0, hh] = st_ref[hh].T


def _hgrn(proj, lbp, ng, s0, *, layer, batch, seq, tc, chunk, heads_per_step):
    m = proj.shape[1]
    nt = seq // tc
    hb = heads_per_step
    n_chunks = tc // chunk
    unroll = min(HGRN_UNROLL, n_chunks) if n_chunks > 1 else min(HGRN_HEADS_UNROLL, hb)

    def sec(cb):
        return pl.BlockSpec((hb, tc, LANES), lambda b, h, t: (cb // hb + h, b * nt + t, 0))

    return pl.pallas_call(
        functools.partial(_hgrn_kernel, layer=layer, chunk=chunk, n_chunks=tc // chunk,
                          heads_per_step=hb, unroll=unroll),
        grid=(batch, HEADS // hb, nt),
        in_specs=[
            pl.BlockSpec((hb,) + lbp.shape[1:], lambda b, h, t: (h, 0, 0)),
            pl.BlockSpec((hb, 1, LANES), lambda b, h, t: (h, 0, 0)),
            sec(CB_Q), sec(CB_F), sec(CB_V), sec(CB_GH),
            pl.BlockSpec((None, 1, hb, HEAD_DIM, HEAD_DIM), lambda b, h, t: (layer, b, h, 0, 0)),
        ],
        out_specs=[
            pl.BlockSpec((hb, tc, LANES), lambda b, h, t: (h, b * nt + t, 0)),
            pl.BlockSpec((1, hb, HEAD_DIM, HEAD_DIM), lambda b, h, t: (b, h, 0, 0)),
        ],
        out_shape=[
            jax.ShapeDtypeStruct((HEADS, m, LANES), BF16),
            jax.ShapeDtypeStruct((batch, HEADS, HEAD_DIM, HEAD_DIM), F32),
        ],
        scratch_shapes=[pltpu.VMEM((hb, HEAD_DIM, HEAD_DIM), F32),
                        pltpu.VMEM((unroll, 3, SUBLANES + chunk, LANES), F32),
                        pltpu.VMEM((unroll if n_chunks == 1 else 1, HEAD_DIM, HEAD_DIM), F32)],
        compiler_params=pltpu.CompilerParams(
            dimension_semantics=("parallel", "parallel", "arbitrary"),
            vmem_limit_bytes=VMEM_LIMIT_BYTES),
        name="hgrn2",
    )(lbp, ng, proj, proj, proj, proj, s0)


def _rglru_kernel(xr_ref, gr_ref, hist_ref, h0_ref, cw_ref, cb_ref, wa_ref, ba_ref, wx_ref, bx_ref,
                  lam_ref, y_ref, conv_out_ref, h_out_ref, xbuf_ref, hcar_ref, a_ref, b_ref, p_ref, l_ref,
                  *, tc, blocks_per_step, stream_start, n_vregs):
    tt = pl.program_id(2)
    last = tt == pl.num_programs(2) - 1
    hist_lo = SUBLANES - (CONV_W - 1)
    n_seg = SUBLANES * n_vregs
    seg = tc // n_seg
    pitch = seg + SCAN_PAD
    sub = lax.broadcasted_iota(jnp.int32, (SUBLANES, LANES), 0)

    @pl.when(tt == 0)
    def _():
        for jj in range(blocks_per_step):
            xbuf_ref[jj, hist_lo:SUBLANES, :] = hist_ref[0, jj]
            hcar_ref[jj] = jnp.broadcast_to(h0_ref[0, jj], (SUBLANES, LANES))

    def block_body(jj, carry):
        x = xr_ref[jj]
        xbuf_ref[jj, SUBLANES:SUBLANES + tc, :] = x
        cw = cw_ref[jj]
        xc = cb_ref[jj] + xbuf_ref[jj, pl.ds(hist_lo, tc), :] * cw[0:1, :]
        for j in range(1, CONV_W):
            xc = xc + xbuf_ref[jj, pl.ds(hist_lo + j, tc), :] * cw[j:j + 1, :]
        tail = x[tc - (CONV_W - 1):tc, :]
        xbuf_ref[jj, hist_lo:SUBLANES, :] = tail

        xcb = xc.astype(BF16)
        r = jax.nn.sigmoid(jnp.dot(xcb, wa_ref[jj], preferred_element_type=F32) + ba_ref[jj])
        gi = jax.nn.sigmoid(jnp.dot(xcb, wx_ref[jj], preferred_element_type=F32) + bx_ref[jj])
        nl = -lam_ref[jj]
        rate = -RG_C * (jnp.maximum(nl, 0.0) + jnp.log1p(jnp.exp(-jnp.abs(nl))))
        log_a = r * rate
        a = jnp.exp2(r * (rate * LOG2E))
        m2 = -jnp.tanh(log_a) * (a * a + 1.0)
        mult = jnp.where(m2 > 0.0, m2 * lax.rsqrt(m2), 0.0)
        if stream_start:
            first = jnp.where((sub == 0) & (tt == 0), 1.0, mult[0:SUBLANES])
            mult = jnp.concatenate([first, mult[SUBLANES:]], axis=0)
        bt = mult * gi * xc
        for s in range(n_seg):
            a_ref[s * pitch:s * pitch + seg, :] = a[s * seg:(s + 1) * seg]
            b_ref[s * pitch:s * pitch + seg, :] = bt[s * seg:(s + 1) * seg]

        def scan_body(i, pl_prev):
            out = []
            for v, (p_prev, l_prev) in enumerate(pl_prev):
                rows = pl.ds(i + v * SUBLANES * pitch, SUBLANES, stride=pitch)
                ai = a_ref[rows, :]
                l_new = ai * l_prev + b_ref[rows, :]
                p_new = ai * p_prev
                p_ref[rows, :] = p_new
                l_ref[rows, :] = l_new
                out.append((p_new, l_new))
            return tuple(out)

        init = tuple((jnp.ones((SUBLANES, LANES), F32), jnp.zeros((SUBLANES, LANES), F32))
                     for _ in range(n_vregs))
        ends = lax.fori_loop(0, seg, scan_body, init, unroll=min(8, seg))
        h_in = hcar_ref[jj]
        pieces = []
        for v, (p_end, l_end) in enumerate(ends):
            for d in (1, 2, 4):
                keep = sub >= d
                l_end = p_end * jnp.where(keep, pltpu.roll(l_end, d, 0), 0.0) + l_end
                p_end = p_end * jnp.where(keep, pltpu.roll(p_end, d, 0), 1.0)
            h_seg_end = p_end * h_in + l_end
            h_seg_start = jnp.where(sub >= 1, pltpu.roll(h_seg_end, 1, 0), h_in)
            h_in = jnp.broadcast_to(h_seg_end[SUBLANES - 1:SUBLANES, :], (SUBLANES, LANES))
            for r in range(SUBLANES):
                lo = (v * SUBLANES + r) * pitch
                pieces.append(p_ref[lo:lo + seg, :] * h_seg_start[r:r + 1, :] + l_ref[lo:lo + seg, :])
        h_last = h_in
        hcar_ref[jj] = h_last
        h = jnp.concatenate(pieces, axis=0)
        y_ref[jj] = (h * _silu(gr_ref[jj])).astype(BF16)

        @pl.when(last)
        def _():
            conv_out_ref[0, jj] = tail
            h_out_ref[0, jj] = h_last[0:1, :]
        return carry

    lax.fori_loop(0, blocks_per_step, block_body, 0)


def _rglru(proj, hist, h0, cw, cb, wa, ba, wx, bx, lam, *, batch, seq, tc, blocks_per_step,
           stream_start):
    m = proj.shape[1]
    nt = seq // tc
    jb = blocks_per_step
    n_vregs = min(SCAN_VREGS, tc // SUBLANES)

    def sec(cb_off):
        return pl.BlockSpec((jb, tc, LANES), lambda b, j, t: (cb_off // jb + j, b * nt + t, 0))

    def per_block(shape):
        return pl.BlockSpec((jb,) + shape, lambda b, j, t: (j,) + (0,) * len(shape))

    return pl.pallas_call(
        functools.partial(_rglru_kernel, tc=tc, blocks_per_step=jb, stream_start=stream_start,
                          n_vregs=n_vregs),
        grid=(batch, RG_BLOCKS // jb, nt),
        in_specs=[
            sec(CB_XR), sec(CB_GR),
            pl.BlockSpec((1, jb, CONV_W - 1, LANES), lambda b, j, t: (b, j, 0, 0)),
            pl.BlockSpec((1, jb, 1, LANES), lambda b, j, t: (b, j, 0, 0)),
            per_block((CONV_W, LANES)), per_block((1, LANES)),
            per_block((LANES, LANES)), per_block((1, LANES)),
            per_block((LANES, LANES)), per_block((1, LANES)),
            per_block((1, LANES)),
        ],
        out_specs=[
            pl.BlockSpec((jb, tc, LANES), lambda b, j, t: (j, b * nt + t, 0)),
            pl.BlockSpec((1, jb, CONV_W - 1, LANES), lambda b, j, t: (b, j, 0, 0)),
            pl.BlockSpec((1, jb, 1, LANES), lambda b, j, t: (b, j, 0, 0)),
        ],
        out_shape=[
            jax.ShapeDtypeStruct((RG_BLOCKS, m, LANES), BF16),
            jax.ShapeDtypeStruct((batch, RG_BLOCKS, CONV_W - 1, LANES), F32),
            jax.ShapeDtypeStruct((batch, RG_BLOCKS, 1, LANES), F32),
        ],
        scratch_shapes=[
            pltpu.VMEM((jb, SUBLANES + tc, LANES), F32),
            pltpu.VMEM((jb, SUBLANES, LANES), F32),
        ] + [pltpu.VMEM((tc + SUBLANES * n_vregs * SCAN_PAD, LANES), F32)] * 4,
        compiler_params=pltpu.CompilerParams(
            dimension_semantics=("parallel", "parallel", "arbitrary"),
            vmem_limit_bytes=VMEM_LIMIT_BYTES),
        name="rglru",
    )(proj, proj, hist, h0, cw, cb, wa, ba, wx, bx, lam)


def _out_kernel(yh_ref, yr_ref, mh_ref, mr_ref, x_ref, wbh_ref, wbr_ref, wo_ref, g_ref, o_ref):
    def gather(ref):
        return jnp.concatenate([ref[c] for c in range(D_MODEL // LANES)], axis=1)

    b_h = jnp.dot(gather(yh_ref), wbh_ref[...], preferred_element_type=F32)
    b_r = jnp.dot(gather(yr_ref), wbr_ref[...], preferred_element_type=F32)
    merged = jax.nn.sigmoid(gather(mh_ref)) * b_h + jax.nn.sigmoid(gather(mr_ref)) * b_r
    out = jnp.dot(merged.astype(BF16), wo_ref[...], preferred_element_type=F32)
    ms = jnp.mean(out * out, axis=-1, keepdims=True)
    o_ref[...] = x_ref[...] + out * lax.rsqrt(ms + EPS) * g_ref[...]


def _out_stage(yh, yr, proj, x, wbh, wbr, wo, g, *, layer, tm):
    m = x.shape[0]
    nb = D_MODEL // LANES

    def resident(shape):
        return pl.BlockSpec(shape, lambda i: (0,) * len(shape), pipeline_mode=pl.Buffered(1))

    stacked_weight = pl.BlockSpec((None, D_MODEL, D_MODEL), lambda i: (layer, 0, 0),
                                  pipeline_mode=pl.Buffered(1))

    return pl.pallas_call(
        _out_kernel,
        grid=(m // tm,),
        in_specs=[
            pl.BlockSpec((nb, tm, LANES), lambda i: (0, i, 0)),
            pl.BlockSpec((nb, tm, LANES), lambda i: (0, i, 0)),
            pl.BlockSpec((nb, tm, LANES), lambda i: (CB_MH // nb, i, 0)),
            pl.BlockSpec((nb, tm, LANES), lambda i: (CB_MR // nb, i, 0)),
            pl.BlockSpec((tm, D_MODEL), lambda i: (i, 0)),
            stacked_weight, stacked_weight, stacked_weight,
            resident((1, D_MODEL)),
        ],
        out_specs=pl.BlockSpec((tm, D_MODEL), lambda i: (i, 0)),
        out_shape=jax.ShapeDtypeStruct((m, D_MODEL), F32),
        compiler_params=pltpu.CompilerParams(
            dimension_semantics=("parallel",), vmem_limit_bytes=VMEM_LIMIT_BYTES),
        name="out_stage",
    )(yh, yr, proj, proj, x, wbh, wbr, wo, g)


def _stream_tiles(batch, seq):
    m = batch * seq
    tm_in = min(m, 1024)
    tn_in = 2048
    tm_out = min(m, 256)
    tc = min(seq, 4096)
    chunk = min(seq, HGRN_CHUNK)
    per_step = 1 if seq >= 1024 else HEADS
    return dict(tm_in=tm_in, tn_in=tn_in, tm_out=tm_out, tc=tc, chunk=chunk, per_step=per_step)


def _run_trunk(x, s_hgrn, conv_hist, h_rg, stream_start, w):
    batch, seq, _ = x.shape
    m = batch * seq
    t = _stream_tiles(batch, seq)
    xf = x.reshape(m, D_MODEL)
    new_s, new_conv, new_h = [], [], []
    for layer in range(w["w_in"].shape[0]):
        proj = _in_proj(xf, w["pre_norm_g"][layer], w["w_in"], layer=layer, tm=t["tm_in"], tn=t["tn_in"])
        yh, s_l = _hgrn(proj, w["lbp"], w["hgrn_norm_g"][layer], s_hgrn, layer=layer,
                        batch=batch, seq=seq, tc=t["tc"], chunk=t["chunk"],
                        heads_per_step=t["per_step"])
        hist = conv_hist[layer].reshape(batch, CONV_W - 1, RG_BLOCKS, LANES).transpose(0, 2, 1, 3)
        h0 = h_rg[layer].reshape(batch, RG_BLOCKS, 1, LANES)
        yr, conv_l, h_l = _rglru(
            proj, hist, h0, w["conv_w"][layer], w["conv_b"][layer], w["w_gate_a"][layer],
            w["b_gate_a"][layer], w["w_gate_x"][layer], w["b_gate_x"][layer], w["rg_lambda"][layer],
            batch=batch, seq=seq, tc=t["tc"], blocks_per_step=t["per_step"],
            stream_start=stream_start)
        xf = _out_stage(yh, yr, proj, xf, w["w_branch_hgrn"], w["w_branch_rglru"], w["w_out"],
                        w["post_norm_g"][layer], layer=layer, tm=t["tm_out"])
        new_s.append(s_l)
        new_conv.append(conv_l.transpose(0, 2, 1, 3).reshape(batch, CONV_W - 1, D_MODEL))
        new_h.append(h_l.reshape(batch, D_MODEL))
    return xf.reshape(batch, seq, D_MODEL), jnp.stack(new_s), jnp.stack(new_conv), jnp.stack(new_h)


def _prep_weights(lb_param, pre_norm_g, w_in, hgrn_norm_g, w_branch_hgrn, conv_w, conv_b, w_gate_a,
                  b_gate_a, w_gate_x, b_gate_x, rg_lambda, w_branch_rglru, w_out, post_norm_g):
    depth = w_in.shape[0]

    def rows(a):
        return a.reshape(depth, 1, D_MODEL)

    def blocks(a):
        return a.reshape(depth, RG_BLOCKS, 1, LANES)

    return dict(
        lbp=lb_param.astype(F32).reshape(depth, HEADS, LANES).transpose(1, 0, 2),
        pre_norm_g=rows(pre_norm_g), post_norm_g=rows(post_norm_g),
        hgrn_norm_g=hgrn_norm_g.reshape(depth, HEADS, 1, LANES),
        w_in=w_in.astype(BF16), w_branch_hgrn=w_branch_hgrn.astype(BF16),
        w_branch_rglru=w_branch_rglru.astype(BF16), w_out=w_out.astype(BF16),
        conv_w=conv_w.reshape(depth, CONV_W, RG_BLOCKS, LANES).transpose(0, 2, 1, 3),
        conv_b=blocks(conv_b), w_gate_a=w_gate_a.astype(BF16), b_gate_a=blocks(b_gate_a),
        w_gate_x=w_gate_x.astype(BF16), b_gate_x=blocks(b_gate_x), rg_lambda=blocks(rg_lambda),
    )


def kernel(x_prompt, x_sample, state_hgrn, state_conv, state_rglru, lb_param, pre_norm_g, w_in,
           hgrn_norm_g, w_branch_hgrn, conv_w, conv_b, w_gate_a, b_gate_a, w_gate_x, b_gate_x,
           rg_lambda, w_branch_rglru, w_out, post_norm_g):
    w = _prep_weights(lb_param, pre_norm_g, w_in, hgrn_norm_g, w_branch_hgrn, conv_w, conv_b,
                      w_gate_a, b_gate_a, w_gate_x, b_gate_x, rg_lambda, w_branch_rglru, w_out,
                      post_norm_g)
    depth = w_in.shape[0]
    nb = x_prompt.shape[0]
    s0 = jnp.zeros((depth, nb, HEADS, HEAD_DIM, HEAD_DIM), F32)
    c0 = jnp.zeros((depth, nb, CONV_W - 1, D_MODEL), F32)
    h0 = jnp.zeros((depth, nb, D_MODEL), F32)
    y_p, s_p, c_p, h_p = _run_trunk(x_prompt, s0, c0, h0, True, w)
    y_s, s_s, c_s, h_s = _run_trunk(x_sample, state_hgrn, state_conv, state_rglru, False, w)
    return (y_p, y_s, s_p, c_p, h_p, s_s, c_s, h_s)
```

```python
import functools

import jax
import jax.numpy as jnp
from jax import lax
from jax.experimental import pallas as pl
from jax.experimental.pallas import tpu as pltpu

F32 = jnp.float32
BF16 = jnp.bfloat16

D_MODEL = 2048
HEADS = 16
HEAD_DIM = 128
RG_BLOCKS = 16
CONV_W = 4
RG_C = 8.0
EPS = 1e-6
LANES = 128
SUBLANES = 8
PROJ = 8 * D_MODEL
N_COL_BLOCKS = PROJ // LANES
CB_Q, CB_F, CB_V, CB_GH, CB_XR, CB_GR, CB_MH, CB_MR = (16 * i for i in range(8))
HGRN_CHUNK = 64
HGRN_UNROLL = 32
LOG2E = 1.4426950408889634
HGRN_HEADS_UNROLL = 16
HGRN_SPAN = 2
HGRN_SAFE_LOG2 = 120.0
SCAN_VREGS = 4
SCAN_PAD = 4
VMEM_LIMIT_BYTES = 56 * 1024 * 1024


def _sigmoid_pair(z):
    e = jnp.exp(-jnp.abs(z))
    inv = 1.0 / (1.0 + e)
    pos = z >= 0
    sig_pos = jnp.where(pos, 1.0, e) * inv
    sig_neg = jnp.where(pos, e, 1.0) * inv
    log_sig = jnp.minimum(z, 0.0) - jnp.log1p(e)
    return sig_pos, sig_neg, log_sig


def _silu(x):
    return x * jax.nn.sigmoid(x)


def _in_proj_kernel(x_ref, g_ref, w_ref, o_ref, u_ref, *, tn):
    @pl.when(pl.program_id(1) == 0)
    def _():
        x = x_ref[...]
        ms = jnp.mean(x * x, axis=-1, keepdims=True)
        u_ref[...] = (x * lax.rsqrt(ms + EPS) * g_ref[...]).astype(BF16)

    res = jnp.dot(u_ref[...], w_ref[...], preferred_element_type=F32)
    for c in range(tn // LANES):
        o_ref[c] = res[:, c * LANES:(c + 1) * LANES]


def _in_proj(x, g, w, *, layer, tm, tn):
    m = x.shape[0]
    return pl.pallas_call(
        functools.partial(_in_proj_kernel, tn=tn),
        grid=(m // tm, PROJ // tn),
        in_specs=[
            pl.BlockSpec((tm, D_MODEL), lambda i, j: (i, 0)),
            pl.BlockSpec((1, D_MODEL), lambda i, j: (0, 0)),
            pl.BlockSpec((None, D_MODEL, tn), lambda i, j: (layer, 0, j)),
        ],
        out_specs=pl.BlockSpec((tn // LANES, tm, LANES), lambda i, j: (j, i, 0)),
        out_shape=jax.ShapeDtypeStruct((N_COL_BLOCKS, m, LANES), F32),
        scratch_shapes=[pltpu.VMEM((tm, D_MODEL), BF16)],
        compiler_params=pltpu.CompilerParams(
            dimension_semantics=("parallel", "arbitrary"), vmem_limit_bytes=VMEM_LIMIT_BYTES),
        name="in_proj",
    )(x, g, w)


def _hgrn_prep(q_raw, z, lb, qkb_ref, shift_ok, *, chunk):
    n_groups = chunk // SUBLANES
    q = _silu(q_raw)
    sig_pos, sig_neg, log_sig = _sigmoid_pair(z)
    if lb is None:
        log_f, k = log_sig, sig_neg
    else:
        log_f = jnp.log(lb + (1.0 - lb) * sig_pos)
        k = (1.0 - lb) * sig_neg
    lf2 = log_f * LOG2E
    b_rows = slice(SUBLANES, SUBLANES + chunk)
    scans = []
    for g in range(n_groups):
        bg = lf2[g * SUBLANES:(g + 1) * SUBLANES]
        for ok, d in zip(shift_ok, (1, 2, 4)):
            bg = bg + jnp.where(ok, pltpu.roll(bg, d, 0), 0.0)
        scans.append(bg)
    qkb_ref[0, b_rows, :] = q
    qkb_ref[1, b_rows, :] = k
    qkb_ref[2, b_rows, :] = jnp.concatenate(scans, axis=0)
    run = None
    for g in range(n_groups):
        lo = SUBLANES + g * SUBLANES
        tot = qkb_ref[2, lo + SUBLANES - 1:lo + SUBLANES, :]
        if run is not None:
            qkb_ref[2, lo:lo + SUBLANES, :] = scans[g] + run
            tot = tot + run
        run = tot
    return run


def _hgrn_scores_exact(qkb_ref, consts, *, chunk):
    diag_c, col_c = consts
    n_groups = chunk // SUBLANES

    def grp(plane, g, shift=0):
        lo = SUBLANES + g * SUBLANES - shift
        return qkb_ref[plane, lo:lo + SUBLANES, :]

    def end_row(g):
        lo = SUBLANES + g * SUBLANES + SUBLANES - 1
        return qkb_ref[2, lo:lo + 1, :]

    p_rows = []
    for g in range(n_groups):
        acc = jnp.zeros((SUBLANES, chunk), F32)
        for d in range(SUBLANES):
            a_d = grp(0, g) * grp(1, g, d)
            if d:
                a_d = a_d * jnp.exp2(grp(2, g) - grp(2, g, d))
            acc = jnp.where(diag_c == g * SUBLANES - d, jnp.sum(a_d, axis=-1, keepdims=True), acc)
        p_rows.append(jnp.where(col_c >= g * SUBLANES, acc, 0.0) if g else acc)

    half = chunk // 2
    while half >= SUBLANES:
        gph = half // SUBLANES
        q_parts, k_parts, q_groups = [], [], []
        for g in range(n_groups):
            first = (g // gph) * gph
            if (g // gph) % 2 == 1:
                q_parts.append(grp(0, g) * jnp.exp2(grp(2, g) - end_row(first - 1)))
                q_groups.append(g)
                k_parts.append(jnp.zeros((SUBLANES, LANES), F32))
            else:
                k_parts.append(grp(1, g) * jnp.exp2(end_row(first + gph - 1) - grp(2, g)))
        ql = jnp.concatenate(q_parts, axis=0).astype(BF16)
        kl = jnp.concatenate(k_parts, axis=0).astype(BF16)
        sc = lax.dot_general(ql, kl, (((1,), (1,)), ((), ())), preferred_element_type=F32)
        for j, g in enumerate(q_groups):
            lo = ((g // gph) - 1) * half
            valid = (col_c >= lo) & (col_c < lo + half)
            p_rows[g] = p_rows[g] + jnp.where(valid, sc[j * SUBLANES:(j + 1) * SUBLANES], 0.0)
        half //= 2
    return jnp.concatenate(p_rows, axis=0)


def _block_diag(blocks):
    n = len(blocks)
    zero = jnp.zeros_like(blocks[0])
    return jnp.concatenate(
        [jnp.concatenate([blk if j == i else zero for j in range(n)], axis=1)
         for i, blk in enumerate(blocks)], axis=0)


def _hgrn_finish(runs):
    chunk = runs[0]["v"].shape[0] // len(runs[0]["qkb"])
    rows = slice(SUBLANES, SUBLANES + chunk)
    staged = []
    for r in runs:
        bs = [ref[2, rows, :] for ref in r["qkb"]]
        ks = [ref[1, rows, :] for ref in r["qkb"]]
        q_ins = [(ref[0, rows, :] * jnp.exp2(b)).astype(BF16) for ref, b in zip(r["qkb"], bs)]
        k_outs = [(k * jnp.exp2(e - b)).astype(BF16) for k, b, e in zip(ks, bs, r["b_ends"])]
        upd = jnp.dot(r["v"].T.astype(BF16), _block_diag(k_outs), preferred_element_type=F32)
        p_mat = r["scores_fn"](q_ins, ks, bs).astype(BF16)
        staged.append((q_ins, upd, p_mat))
    states, finals = [], []
    st = None
    for r, (q_ins, upd, p_mat) in zip(runs, staged):
        st = st if r["st"] is None else r["st"]
        run_states = []
        for c, e in enumerate(r["b_ends"]):
            run_states.append(st)
            st = st * jnp.exp2(e) + upd[:, c * LANES:(c + 1) * LANES]
        states.append(jnp.concatenate(run_states, axis=1).astype(BF16))
        finals.append(st)
    outs = []
    for r, (q_ins, upd, p_mat), s_cat in zip(runs, staged, states):
        o = jnp.dot(p_mat, r["v"].astype(BF16), preferred_element_type=F32)
        o = o + lax.dot_general(_block_diag(q_ins), s_cat, (((1,), (1,)), ((), ())),
                                preferred_element_type=F32)
        o = o * lax.rsqrt(jnp.mean(o * o, axis=-1, keepdims=True) + EPS)
        outs.append(o * r["ng"] * _silu(r["g_h"]))
    return outs, finals


def _lower_bound(lbp, layer):
    if layer == 0:
        return None
    ex = jnp.exp(lbp - jnp.max(lbp, axis=0, keepdims=True))
    p = ex / jnp.sum(ex, axis=0, keepdims=True)
    return jnp.sum(p[1:layer + 1], axis=0, keepdims=True)


def _hgrn_kernel(lbp_ref, ng_ref, q_ref, f_ref, v_ref, g_ref, s0_ref, y_ref, s_out_ref, st_ref, qkb_ref,
                 stb_ref,
                 *, layer, chunk, n_chunks, heads_per_step, unroll):
    tt = pl.program_id(2)

    @pl.when(tt == 0)
    def _():
        for hh in range(heads_per_step):
            st_ref[hh] = s0_ref[0, hh].T

    for u in range(unroll):
        qkb_ref[u, :, 0:SUBLANES, :] = jnp.zeros((3, SUBLANES, LANES), F32)
    sub = lax.broadcasted_iota(jnp.int32, (SUBLANES, LANES), 0)
    shift_ok = [sub >= d for d in (1, 2, 4)]
    across_heads = n_chunks == 1
    span = 1 if across_heads else min(HGRN_SPAN, unroll)
    n_iter = (heads_per_step if across_heads else n_chunks) // unroll

    def body(idx, carry):
        if across_heads:
            heads = [idx * unroll + u for u in range(unroll)]
            base = 0
        else:
            heads = [idx // n_iter] * unroll
            base = (idx % n_iter) * (unroll * chunk)

        def rows(u, n=1):
            start = 0 if across_heads else pl.multiple_of(base + u * chunk, chunk)
            return pl.ds(start, n * chunk)

        b_ends = [_hgrn_prep(q_ref[heads[u], rows(u), :], f_ref[heads[u], rows(u), :],
                             _lower_bound(lbp_ref[heads[u]], layer), qkb_ref.at[u], shift_ok,
                             chunk=chunk) for u in range(unroll)]
        lowest = functools.reduce(jnp.minimum, b_ends)
        safe = jnp.min(lowest, axis=1, keepdims=True)[0, 0] >= -HGRN_SAFE_LOG2

        def run(scores_fn, state_of):
            starts = list(range(0, unroll, span))
            runs = [dict(qkb=[qkb_ref.at[u + c] for c in range(span)], b_ends=b_ends[u:u + span],
                         v=v_ref[heads[u], rows(u, span), :], g_h=g_ref[heads[u], rows(u, span), :],
                         ng=ng_ref[heads[u]], scores_fn=scores_fn(u),
                         st=state_of(u) if (across_heads or u == 0) else None)
                    for u in starts]
            outs, finals = _hgrn_finish(runs)
            for u, y in zip(starts, outs):
                y_ref[heads[u], rows(u, span), :] = y.astype(BF16)
            if across_heads:
                for u, st in zip(starts, finals):
                    st_ref[heads[u]] = st
            else:
                st_ref[heads[0]] = finals[-1]

        n_saved = unroll if across_heads else 1
        entering = [st_ref[heads[u]] for u in range(n_saved)]
        for u in range(n_saved):
            stb_ref[u] = entering[u]
        t_idx = lax.broadcasted_iota(jnp.int32, (span * chunk, span * chunk), 0)
        s_idx = lax.broadcasted_iota(jnp.int32, (span * chunk, span * chunk), 1)
        keep = s_idx <= t_idx
        for c in range(1, span):
            keep = keep & ((s_idx >= c * chunk) | (t_idx < c * chunk))

        def fast_scores(u):
            def fn(q_ins, ks, bs):
                k_rel = jnp.concatenate([(k * jnp.exp2(-b)).astype(BF16) for k, b in zip(ks, bs)],
                                        axis=0)
                sc = lax.dot_general(jnp.concatenate(q_ins, axis=0), k_rel,
                                     (((1,), (1,)), ((), ())), preferred_element_type=F32)
                return jnp.where(keep, sc, 0.0)
            return fn
        run(fast_scores, lambda u: entering[u])

        @pl.when(jnp.logical_not(safe))
        def _():
            sub_c = lax.broadcasted_iota(jnp.int32, (SUBLANES, chunk), 0)
            col_c = lax.broadcasted_iota(jnp.int32, (SUBLANES, chunk), 1)
            consts = (col_c - sub_c, col_c)

            def exact_scores(u):
                return lambda q_ins, ks, bs: _block_diag(
                    [_hgrn_scores_exact(qkb_ref.at[u + c], consts, chunk=chunk) for c in range(span)])
            run(exact_scores, lambda u: stb_ref[u])
        return carry

    lax.fori_loop(0, n_iter if across_heads else heads_per_step * n_iter, body, 0)

    @pl.when(tt == pl.num_programs(2) - 1)
    def _():
        for hh in range(heads_per_step):
            s_out_ref[0, hh] = st_ref[hh].T


def _hgrn(proj, lbp, ng, s0, *, layer, batch, seq, tc, chunk, heads_per_step):
    m = proj.shape[1]
    nt = seq // tc
    hb = heads_per_step
    n_chunks = tc // chunk
    unroll = min(HGRN_UNROLL, n_chunks) if n_chunks > 1 else min(HGRN_HEADS_UNROLL, hb)

    def sec(cb):
        return pl.BlockSpec((hb, tc, LANES), lambda b, h, t: (cb // hb + h, b * nt + t, 0))

    return pl.pallas_call(
        functools.partial(_hgrn_kernel, layer=layer, chunk=chunk, n_chunks=tc // chunk,
                          heads_per_step=hb, unroll=unroll),
        grid=(batch, HEADS // hb, nt),
        in_specs=[
            pl.BlockSpec((hb,) + lbp.shape[1:], lambda b, h, t: (h, 0, 0)),
            pl.BlockSpec((hb, 1, LANES), lambda b, h, t: (h, 0, 0)),
            sec(CB_Q), sec(CB_F), sec(CB_V), sec(CB_GH),
            pl.BlockSpec((None, 1, hb, HEAD_DIM, HEAD_DIM), lambda b, h, t: (layer, b, h, 0, 0)),
        ],
        out_specs=[
            pl.BlockSpec((hb, tc, LANES), lambda b, h, t: (h, b * nt + t, 0)),
            pl.BlockSpec((1, hb, HEAD_DIM, HEAD_DIM), lambda b, h, t: (b, h, 0, 0)),
        ],
        out_shape=[
            jax.ShapeDtypeStruct((HEADS, m, LANES), BF16),
            jax.ShapeDtypeStruct((batch, HEADS, HEAD_DIM, HEAD_DIM), F32),
        ],
        scratch_shapes=[pltpu.VMEM((hb, HEAD_DIM, HEAD_DIM), F32),
                        pltpu.VMEM((unroll, 3, SUBLANES + chunk, LANES), F32),
                        pltpu.VMEM((unroll if n_chunks == 1 else 1, HEAD_DIM, HEAD_DIM), F32)],
        compiler_params=pltpu.CompilerParams(
            dimension_semantics=("parallel", "parallel", "arbitrary"),
            vmem_limit_bytes=VMEM_LIMIT_BYTES),
        name="hgrn2",
    )(lbp, ng, proj, proj, proj, proj, s0)


def _rglru_kernel(xr_ref, gr_ref, hist_ref, h0_ref, cw_ref, cb_ref, wa_ref, ba_ref, wx_ref, bx_ref,
                  lam_ref, y_ref, conv_out_ref, h_out_ref, xbuf_ref, hcar_ref, a_ref, b_ref, p_ref, l_ref,
                  *, tc, blocks_per_step, stream_start, n_vregs):
    tt = pl.program_id(2)
    last = tt == pl.num_programs(2) - 1
    hist_lo = SUBLANES - (CONV_W - 1)
    n_seg = SUBLANES * n_vregs
    seg = tc // n_seg
    pitch = seg + SCAN_PAD
    sub = lax.broadcasted_iota(jnp.int32, (SUBLANES, LANES), 0)

    @pl.when(tt == 0)
    def _():
        for jj in range(blocks_per_step):
            xbuf_ref[jj, hist_lo:SUBLANES, :] = hist_ref[0, jj]
            hcar_ref[jj] = jnp.broadcast_to(h0_ref[0, jj], (SUBLANES, LANES))

    def block_body(jj):
        x = xr_ref[jj]
        xbuf_ref[jj, SUBLANES:SUBLANES + tc, :] = x
        cw = cw_ref[jj]
        xc = cb_ref[jj] + xbuf_ref[jj, pl.ds(hist_lo, tc), :] * cw[0:1, :]
        for j in range(1, CONV_W):
            xc = xc + xbuf_ref[jj, pl.ds(hist_lo + j, tc), :] * cw[j:j + 1, :]
        tail = x[tc - (CONV_W - 1):tc, :]
        xbuf_ref[jj, hist_lo:SUBLANES, :] = tail

        xcb = xc.astype(BF16)
        r = jax.nn.sigmoid(jnp.dot(xcb, wa_ref[jj], preferred_element_type=F32) + ba_ref[jj])
        gi = jax.nn.sigmoid(jnp.dot(xcb, wx_ref[jj], preferred_element_type=F32) + bx_ref[jj])
        nl = -lam_ref[jj]
        rate = -RG_C * (jnp.maximum(nl, 0.0) + jnp.log1p(jnp.exp(-jnp.abs(nl))))
        log_a = r * rate
        a = jnp.exp2(r * (rate * LOG2E))
        m2 = -jnp.tanh(log_a) * (a * a + 1.0)
        mult = jnp.where(m2 > 0.0, m2 * lax.rsqrt(m2), 0.0)
        if stream_start:
            first = jnp.where((sub == 0) & (tt == 0), 1.0, mult[0:SUBLANES])
            mult = jnp.concatenate([first, mult[SUBLANES:]], axis=0)
        bt = mult * gi * xc
        for s in range(n_seg):
            a_ref[jj, s * pitch:s * pitch + seg, :] = a[s * seg:(s + 1) * seg]
            b_ref[jj, s * pitch:s * pitch + seg, :] = bt[s * seg:(s + 1) * seg]

        def scan_body(i, pl_prev):
            out = []
            for v, (p_prev, l_prev) in enumerate(pl_prev):
                rows = pl.ds(i + v * SUBLANES * pitch, SUBLANES, stride=pitch)
                ai = a_ref.at[jj][rows, :]
                l_new = ai * l_prev + b_ref.at[jj][rows, :]
                p_new = ai * p_prev
                p_ref.at[jj][rows, :] = p_new
                l_ref.at[jj][rows, :] = l_new
                out.append((p_new, l_new))
            return tuple(out)

        init = tuple((jnp.ones((SUBLANES, LANES), F32), jnp.zeros((SUBLANES, LANES), F32))
                     for _ in range(n_vregs))
        ends = lax.fori_loop(0, seg, scan_body, init, unroll=min(8, seg))
        h_in = hcar_ref[jj]
        pieces = []
        for v, (p_end, l_end) in enumerate(ends):
            for d in (1, 2, 4):
                keep = sub >= d
                l_end = p_end * jnp.where(keep, pltpu.roll(l_end, d, 0), 0.0) + l_end
                p_end = p_end * jnp.where(keep, pltpu.roll(p_end, d, 0), 1.0)
            h_seg_end = p_end * h_in + l_end
            h_seg_start = jnp.where(sub >= 1, pltpu.roll(h_seg_end, 1, 0), h_in)
            h_in = jnp.broadcast_to(h_seg_end[SUBLANES - 1:SUBLANES, :], (SUBLANES, LANES))
            for r in range(SUBLANES):
                lo = (v * SUBLANES + r) * pitch
                pieces.append(p_ref[jj, lo:lo + seg, :] * h_seg_start[r:r + 1, :]
                              + l_ref[jj, lo:lo + seg, :])
        h_last = h_in
        hcar_ref[jj] = h_last
        h = jnp.concatenate(pieces, axis=0)
        y_ref[jj] = (h * _silu(gr_ref[jj])).astype(BF16)

        @pl.when(last)
        def _():
            conv_out_ref[0, jj] = tail
            h_out_ref[0, jj] = h_last[0:1, :]

    for jj in range(blocks_per_step):
        block_body(jj)


def _rglru(proj, hist, h0, cw, cb, wa, ba, wx, bx, lam, *, batch, seq, tc, blocks_per_step,
           stream_start):
    m = proj.shape[1]
    nt = seq // tc
    jb = blocks_per_step
    n_vregs = min(SCAN_VREGS, tc // SUBLANES)

    def sec(cb_off):
        return pl.BlockSpec((jb, tc, LANES), lambda b, j, t: (cb_off // jb + j, b * nt + t, 0))

    def per_block(shape):
        return pl.BlockSpec((jb,) + shape, lambda b, j, t: (j,) + (0,) * len(shape))

    return pl.pallas_call(
        functools.partial(_rglru_kernel, tc=tc, blocks_per_step=jb, stream_start=stream_start,
                          n_vregs=n_vregs),
        grid=(batch, RG_BLOCKS // jb, nt),
        in_specs=[
            sec(CB_XR), sec(CB_GR),
            pl.BlockSpec((1, jb, CONV_W - 1, LANES), lambda b, j, t: (b, j, 0, 0)),
            pl.BlockSpec((1, jb, 1, LANES), lambda b, j, t: (b, j, 0, 0)),
            per_block((CONV_W, LANES)), per_block((1, LANES)),
            per_block((LANES, LANES)), per_block((1, LANES)),
            per_block((LANES, LANES)), per_block((1, LANES)),
            per_block((1, LANES)),
        ],
        out_specs=[
            pl.BlockSpec((jb, tc, LANES), lambda b, j, t: (j, b * nt + t, 0)),
            pl.BlockSpec((1, jb, CONV_W - 1, LANES), lambda b, j, t: (b, j, 0, 0)),
            pl.BlockSpec((1, jb, 1, LANES), lambda b, j, t: (b, j, 0, 0)),
        ],
        out_shape=[
            jax.ShapeDtypeStruct((RG_BLOCKS, m, LANES), BF16),
            jax.ShapeDtypeStruct((batch, RG_BLOCKS, CONV_W - 1, LANES), F32),
            jax.ShapeDtypeStruct((batch, RG_BLOCKS, 1, LANES), F32),
        ],
        scratch_shapes=[
            pltpu.VMEM((jb, SUBLANES + tc, LANES), F32),
            pltpu.VMEM((jb, SUBLANES, LANES), F32),
        ] + [pltpu.VMEM((jb, tc + SUBLANES * n_vregs * SCAN_PAD, LANES), F32)] * 4,
        compiler_params=pltpu.CompilerParams(
            dimension_semantics=("parallel", "parallel", "arbitrary"),
            vmem_limit_bytes=VMEM_LIMIT_BYTES),
        name="rglru",
    )(proj, proj, hist, h0, cw, cb, wa, ba, wx, bx, lam)


def _out_kernel(yh_ref, yr_ref, mh_ref, mr_ref, x_ref, wbh_ref, wbr_ref, wo_ref, g_ref, o_ref):
    def gather(ref):
        return jnp.concatenate([ref[c] for c in range(D_MODEL // LANES)], axis=1)

    b_h = jnp.dot(gather(yh_ref), wbh_ref[...], preferred_element_type=F32)
    b_r = jnp.dot(gather(yr_ref), wbr_ref[...], preferred_element_type=F32)
    merged = jax.nn.sigmoid(gather(mh_ref)) * b_h + jax.nn.sigmoid(gather(mr_ref)) * b_r
    out = jnp.dot(merged.astype(BF16), wo_ref[...], preferred_element_type=F32)
    ms = jnp.mean(out * out, axis=-1, keepdims=True)
    o_ref[...] = x_ref[...] + out * lax.rsqrt(ms + EPS) * g_ref[...]


def _out_stage(yh, yr, proj, x, wbh, wbr, wo, g, *, layer, tm):
    m = x.shape[0]
    nb = D_MODEL // LANES

    def resident(shape):
        return pl.BlockSpec(shape, lambda i: (0,) * len(shape), pipeline_mode=pl.Buffered(1))

    stacked_weight = pl.BlockSpec((None, D_MODEL, D_MODEL), lambda i: (layer, 0, 0),
                                  pipeline_mode=pl.Buffered(1))

    return pl.pallas_call(
        _out_kernel,
        grid=(m // tm,),
        in_specs=[
            pl.BlockSpec((nb, tm, LANES), lambda i: (0, i, 0)),
            pl.BlockSpec((nb, tm, LANES), lambda i: (0, i, 0)),
            pl.BlockSpec((nb, tm, LANES), lambda i: (CB_MH // nb, i, 0)),
            pl.BlockSpec((nb, tm, LANES), lambda i: (CB_MR // nb, i, 0)),
            pl.BlockSpec((tm, D_MODEL), lambda i: (i, 0)),
            stacked_weight, stacked_weight, stacked_weight,
            resident((1, D_MODEL)),
        ],
        out_specs=pl.BlockSpec((tm, D_MODEL), lambda i: (i, 0)),
        out_shape=jax.ShapeDtypeStruct((m, D_MODEL), F32),
        compiler_params=pltpu.CompilerParams(
            dimension_semantics=("parallel",), vmem_limit_bytes=VMEM_LIMIT_BYTES),
        name="out_stage",
    )(yh, yr, proj, proj, x, wbh, wbr, wo, g)


def _stream_tiles(batch, seq):
    m = batch * seq
    tm_in = min(m, 1024)
    tn_in = 2048
    tm_out = min(m, 256)
    tc = min(seq, 4096)
    chunk = min(seq, HGRN_CHUNK)
    per_step = 1 if seq >= 1024 else HEADS
    return dict(tm_in=tm_in, tn_in=tn_in, tm_out=tm_out, tc=tc, chunk=chunk, per_step=per_step)


def _run_trunk(x, s_hgrn, conv_hist, h_rg, stream_start, w):
    batch, seq, _ = x.shape
    m = batch * seq
    t = _stream_tiles(batch, seq)
    xf = x.reshape(m, D_MODEL)
    new_s, new_conv, new_h = [], [], []
    for layer in range(w["w_in"].shape[0]):
        proj = _in_proj(xf, w["pre_norm_g"][layer], w["w_in"], layer=layer, tm=t["tm_in"], tn=t["tn_in"])
        yh, s_l = _hgrn(proj, w["lbp"], w["hgrn_norm_g"][layer], s_hgrn, layer=layer,
                        batch=batch, seq=seq, tc=t["tc"], chunk=t["chunk"],
                        heads_per_step=t["per_step"])
        hist = conv_hist[layer].reshape(batch, CONV_W - 1, RG_BLOCKS, LANES).transpose(0, 2, 1, 3)
        h0 = h_rg[layer].reshape(batch, RG_BLOCKS, 1, LANES)
        yr, conv_l, h_l = _rglru(
            proj, hist, h0, w["conv_w"][layer], w["conv_b"][layer], w["w_gate_a"][layer],
            w["b_gate_a"][layer], w["w_gate_x"][layer], w["b_gate_x"][layer], w["rg_lambda"][layer],
            batch=batch, seq=seq, tc=t["tc"], blocks_per_step=t["per_step"],
            stream_start=stream_start)
        xf = _out_stage(yh, yr, proj, xf, w["w_branch_hgrn"], w["w_branch_rglru"], w["w_out"],
                        w["post_norm_g"][layer], layer=layer, tm=t["tm_out"])
        new_s.append(s_l)
        new_conv.append(conv_l.transpose(0, 2, 1, 3).reshape(batch, CONV_W - 1, D_MODEL))
        new_h.append(h_l.reshape(batch, D_MODEL))
    return xf.reshape(batch, seq, D_MODEL), jnp.stack(new_s), jnp.stack(new_conv), jnp.stack(new_h)


def _prep_weights(lb_param, pre_norm_g, w_in, hgrn_norm_g, w_branch_hgrn, conv_w, conv_b, w_gate_a,
                  b_gate_a, w_gate_x, b_gate_x, rg_lambda, w_branch_rglru, w_out, post_norm_g):
    depth = w_in.shape[0]

    def rows(a):
        return a.reshape(depth, 1, D_MODEL)

    def blocks(a):
        return a.reshape(depth, RG_BLOCKS, 1, LANES)

    return dict(
        lbp=lb_param.astype(F32).reshape(depth, HEADS, LANES).transpose(1, 0, 2),
        pre_norm_g=rows(pre_norm_g), post_norm_g=rows(post_norm_g),
        hgrn_norm_g=hgrn_norm_g.reshape(depth, HEADS, 1, LANES),
        w_in=w_in.astype(BF16), w_branch_hgrn=w_branch_hgrn.astype(BF16),
        w_branch_rglru=w_branch_rglru.astype(BF16), w_out=w_out.astype(BF16),
        conv_w=conv_w.reshape(depth, CONV_W, RG_BLOCKS, LANES).transpose(0, 2, 1, 3),
        conv_b=blocks(conv_b), w_gate_a=w_gate_a.astype(BF16), b_gate_a=blocks(b_gate_a),
        w_gate_x=w_gate_x.astype(BF16), b_gate_x=blocks(b_gate_x), rg_lambda=blocks(rg_lambda),
    )


def kernel(x_prompt, x_sample, state_hgrn, state_conv, state_rglru, lb_param, pre_norm_g, w_in,
           hgrn_norm_g, w_branch_hgrn, conv_w, conv_b, w_gate_a, b_gate_a, w_gate_x, b_gate_x,
           rg_lambda, w_branch_rglru, w_out, post_norm_g):
    w = _prep_weights(lb_param, pre_norm_g, w_in, hgrn_norm_g, w_branch_hgrn, conv_w, conv_b,
                      w_gate_a, b_gate_a, w_gate_x, b_gate_x, rg_lambda, w_branch_rglru, w_out,
                      post_norm_g)
    depth = w_in.shape[0]
    nb = x_prompt.shape[0]
    s0 = jnp.zeros((depth, nb, HEADS, HEAD_DIM, HEAD_DIM), F32)
    c0 = jnp.zeros((depth, nb, CONV_W - 1, D_MODEL), F32)
    h0 = jnp.zeros((depth, nb, D_MODEL), F32)
    y_p, s_p, c_p, h_p = _run_trunk(x_prompt, s0, c0, h0, True, w)
    y_s, s_s, c_s, h_s = _run_trunk(x_sample, state_hgrn, state_conv, state_rglru, False, w)
    return (y_p, y_s, s_p, c_p, h_p, s_s, c_s, h_s)
```

```python
import functools

import jax
import jax.numpy as jnp
from jax import lax
from jax.experimental import pallas as pl
from jax.experimental.pallas import tpu as pltpu

F32 = jnp.float32
BF16 = jnp.bfloat16

D_MODEL = 2048
HEADS = 16
HEAD_DIM = 128
RG_BLOCKS = 16
CONV_W = 4
RG_C = 8.0
EPS = 1e-6
LANES = 128
SUBLANES = 8
PROJ = 8 * D_MODEL
N_COL_BLOCKS = PROJ // LANES
CB_Q, CB_F, CB_V, CB_GH, CB_XR, CB_GR, CB_MH, CB_MR = (16 * i for i in range(8))
HGRN_CHUNK = 64
HGRN_UNROLL = 32
LOG2E = 1.4426950408889634
HGRN_HEADS_UNROLL = 16
HGRN_SPAN = 2
HGRN_SAFE_LOG2 = 120.0
SCAN_VREGS = 4
SCAN_PAD = 4
FUSED_TM = 1024
FUSED_PIECE = 256
VMEM_LIMIT_BYTES = 56 * 1024 * 1024


def _sigmoid_pair(z):
    e = jnp.exp(-jnp.abs(z))
    inv = 1.0 / (1.0 + e)
    pos = z >= 0
    sig_pos = jnp.where(pos, 1.0, e) * inv
    sig_neg = jnp.where(pos, e, 1.0) * inv
    log_sig = jnp.minimum(z, 0.0) - jnp.log1p(e)
    return sig_pos, sig_neg, log_sig


def _silu(x):
    return x * jax.nn.sigmoid(x)


def _in_proj_kernel(x_ref, g_ref, w_ref, o_ref, u_ref, *, tn):
    @pl.when(pl.program_id(1) == 0)
    def _():
        x = x_ref[...]
        ms = jnp.mean(x * x, axis=-1, keepdims=True)
        u_ref[...] = (x * lax.rsqrt(ms + EPS) * g_ref[...]).astype(BF16)

    res = jnp.dot(u_ref[...], w_ref[...], preferred_element_type=F32)
    for c in range(tn // LANES):
        o_ref[c] = res[:, c * LANES:(c + 1) * LANES]


def _in_proj(x, g, w, *, layer, tm, tn, col0=0, n_cols=PROJ):
    m = x.shape[0]
    return pl.pallas_call(
        functools.partial(_in_proj_kernel, tn=tn),
        grid=(m // tm, n_cols // tn),
        in_specs=[
            pl.BlockSpec((tm, D_MODEL), lambda i, j: (i, 0)),
            pl.BlockSpec((1, D_MODEL), lambda i, j: (0, 0)),
            pl.BlockSpec((None, D_MODEL, tn), lambda i, j: (layer, 0, col0 // tn + j)),
        ],
        out_specs=pl.BlockSpec((tn // LANES, tm, LANES), lambda i, j: (j, i, 0)),
        out_shape=jax.ShapeDtypeStruct((n_cols // LANES, m, LANES), F32),
        scratch_shapes=[pltpu.VMEM((tm, D_MODEL), BF16)],
        compiler_params=pltpu.CompilerParams(
            dimension_semantics=("parallel", "arbitrary"), vmem_limit_bytes=VMEM_LIMIT_BYTES),
        name="in_proj",
    )(x, g, w)


def _hgrn_prep(q_raw, z, lb, qkb_ref, shift_ok, *, chunk):
    n_groups = chunk // SUBLANES
    q = _silu(q_raw)
    sig_pos, sig_neg, log_sig = _sigmoid_pair(z)
    if lb is None:
        log_f, k = log_sig, sig_neg
    else:
        log_f = jnp.log(lb + (1.0 - lb) * sig_pos)
        k = (1.0 - lb) * sig_neg
    lf2 = log_f * LOG2E
    b_rows = slice(SUBLANES, SUBLANES + chunk)
    scans = []
    for g in range(n_groups):
        bg = lf2[g * SUBLANES:(g + 1) * SUBLANES]
        for ok, d in zip(shift_ok, (1, 2, 4)):
            bg = bg + jnp.where(ok, pltpu.roll(bg, d, 0), 0.0)
        scans.append(bg)
    qkb_ref[0, b_rows, :] = q
    qkb_ref[1, b_rows, :] = k
    qkb_ref[2, b_rows, :] = jnp.concatenate(scans, axis=0)
    run = None
    for g in range(n_groups):
        lo = SUBLANES + g * SUBLANES
        tot = qkb_ref[2, lo + SUBLANES - 1:lo + SUBLANES, :]
        if run is not None:
            qkb_ref[2, lo:lo + SUBLANES, :] = scans[g] + run
            tot = tot + run
        run = tot
    return run


def _hgrn_scores_exact(qkb_ref, consts, *, chunk):
    diag_c, col_c = consts
    n_groups = chunk // SUBLANES

    def grp(plane, g, shift=0):
        lo = SUBLANES + g * SUBLANES - shift
        return qkb_ref[plane, lo:lo + SUBLANES, :]

    def end_row(g):
        lo = SUBLANES + g * SUBLANES + SUBLANES - 1
        return qkb_ref[2, lo:lo + 1, :]

    p_rows = []
    for g in range(n_groups):
        acc = jnp.zeros((SUBLANES, chunk), F32)
        for d in range(SUBLANES):
            a_d = grp(0, g) * grp(1, g, d)
            if d:
                a_d = a_d * jnp.exp2(grp(2, g) - grp(2, g, d))
            acc = jnp.where(diag_c == g * SUBLANES - d, jnp.sum(a_d, axis=-1, keepdims=True), acc)
        p_rows.append(jnp.where(col_c >= g * SUBLANES, acc, 0.0) if g else acc)

    half = chunk // 2
    while half >= SUBLANES:
        gph = half // SUBLANES
        q_parts, k_parts, q_groups = [], [], []
        for g in range(n_groups):
            first = (g // gph) * gph
            if (g // gph) % 2 == 1:
                q_parts.append(grp(0, g) * jnp.exp2(grp(2, g) - end_row(first - 1)))
                q_groups.append(g)
                k_parts.append(jnp.zeros((SUBLANES, LANES), F32))
            else:
                k_parts.append(grp(1, g) * jnp.exp2(end_row(first + gph - 1) - grp(2, g)))
        ql = jnp.concatenate(q_parts, axis=0).astype(BF16)
        kl = jnp.concatenate(k_parts, axis=0).astype(BF16)
        sc = lax.dot_general(ql, kl, (((1,), (1,)), ((), ())), preferred_element_type=F32)
        for j, g in enumerate(q_groups):
            lo = ((g // gph) - 1) * half
            valid = (col_c >= lo) & (col_c < lo + half)
            p_rows[g] = p_rows[g] + jnp.where(valid, sc[j * SUBLANES:(j + 1) * SUBLANES], 0.0)
        half //= 2
    return jnp.concatenate(p_rows, axis=0)


def _block_diag(blocks):
    n = len(blocks)
    zero = jnp.zeros_like(blocks[0])
    return jnp.concatenate(
        [jnp.concatenate([blk if j == i else zero for j in range(n)], axis=1)
         for i, blk in enumerate(blocks)], axis=0)


def _hgrn_finish(runs):
    chunk = runs[0]["v"].shape[0] // len(runs[0]["qkb"])
    rows = slice(SUBLANES, SUBLANES + chunk)
    staged = []
    for r in runs:
        bs = [ref[2, rows, :] for ref in r["qkb"]]
        ks = [ref[1, rows, :] for ref in r["qkb"]]
        q_ins = [(ref[0, rows, :] * jnp.exp2(b)).astype(BF16) for ref, b in zip(r["qkb"], bs)]
        k_outs = [(k * jnp.exp2(e - b)).astype(BF16) for k, b, e in zip(ks, bs, r["b_ends"])]
        upd = jnp.dot(r["v"].T.astype(BF16), _block_diag(k_outs), preferred_element_type=F32)
        p_mat = r["scores_fn"](q_ins, ks, bs).astype(BF16)
        staged.append((q_ins, upd, p_mat))
    states, finals = [], []
    st = None
    for r, (q_ins, upd, p_mat) in zip(runs, staged):
        st = st if r["st"] is None else r["st"]
        run_states = []
        for c, e in enumerate(r["b_ends"]):
            run_states.append(st)
            st = st * jnp.exp2(e) + upd[:, c * LANES:(c + 1) * LANES]
        states.append(jnp.concatenate(run_states, axis=1).astype(BF16))
        finals.append(st)
    outs = []
    for r, (q_ins, upd, p_mat), s_cat in zip(runs, staged, states):
        o = jnp.dot(p_mat, r["v"].astype(BF16), preferred_element_type=F32)
        o = o + lax.dot_general(_block_diag(q_ins), s_cat, (((1,), (1,)), ((), ())),
                                preferred_element_type=F32)
        o = o * lax.rsqrt(jnp.mean(o * o, axis=-1, keepdims=True) + EPS)
        outs.append(o * r["ng"] * _silu(r["g_h"]))
    return outs, finals


def _lower_bound(lbp, layer):
    if layer == 0:
        return None
    ex = jnp.exp(lbp - jnp.max(lbp, axis=0, keepdims=True))
    p = ex / jnp.sum(ex, axis=0, keepdims=True)
    return jnp.sum(p[1:layer + 1], axis=0, keepdims=True)


def _hgrn_kernel(lbp_ref, ng_ref, q_ref, f_ref, v_ref, g_ref, s0_ref, y_ref, s_out_ref, st_ref, qkb_ref,
                 stb_ref,
                 *, layer, chunk, n_chunks, heads_per_step, unroll):
    tt = pl.program_id(2)

    @pl.when(tt == 0)
    def _():
        for hh in range(heads_per_step):
            st_ref[hh] = s0_ref[0, hh].T

    for u in range(unroll):
        qkb_ref[u, :, 0:SUBLANES, :] = jnp.zeros((3, SUBLANES, LANES), F32)
    sub = lax.broadcasted_iota(jnp.int32, (SUBLANES, LANES), 0)
    shift_ok = [sub >= d for d in (1, 2, 4)]
    across_heads = n_chunks == 1
    span = 1 if across_heads else min(HGRN_SPAN, unroll)
    n_iter = (heads_per_step if across_heads else n_chunks) // unroll

    def body(idx, carry):
        if across_heads:
            heads = [idx * unroll + u for u in range(unroll)]
            base = 0
        else:
            heads = [idx // n_iter] * unroll
            base = (idx % n_iter) * (unroll * chunk)

        def rows(u, n=1):
            start = 0 if across_heads else pl.multiple_of(base + u * chunk, chunk)
            return pl.ds(start, n * chunk)

        b_ends = [_hgrn_prep(q_ref[heads[u], rows(u), :], f_ref[heads[u], rows(u), :],
                             _lower_bound(lbp_ref[heads[u]], layer), qkb_ref.at[u], shift_ok,
                             chunk=chunk) for u in range(unroll)]
        lowest = functools.reduce(jnp.minimum, b_ends)
        safe = jnp.min(lowest, axis=1, keepdims=True)[0, 0] >= -HGRN_SAFE_LOG2

        def run(scores_fn, state_of):
            starts = list(range(0, unroll, span))
            runs = [dict(qkb=[qkb_ref.at[u + c] for c in range(span)], b_ends=b_ends[u:u + span],
                         v=v_ref[heads[u], rows(u, span), :], g_h=g_ref[heads[u], rows(u, span), :],
                         ng=ng_ref[heads[u]], scores_fn=scores_fn(u),
                         st=state_of(u) if (across_heads or u == 0) else None)
                    for u in starts]
            outs, finals = _hgrn_finish(runs)
            for u, y in zip(starts, outs):
                y_ref[heads[u], rows(u, span), :] = y.astype(BF16)
            if across_heads:
                for u, st in zip(starts, finals):
                    st_ref[heads[u]] = st
            else:
                st_ref[heads[0]] = finals[-1]

        n_saved = unroll if across_heads else 1
        entering = [st_ref[heads[u]] for u in range(n_saved)]
        for u in range(n_saved):
            stb_ref[u] = entering[u]
        t_idx = lax.broadcasted_iota(jnp.int32, (span * chunk, span * chunk), 0)
        s_idx = lax.broadcasted_iota(jnp.int32, (span * chunk, span * chunk), 1)
        keep = s_idx <= t_idx
        for c in range(1, span):
            keep = keep & ((s_idx >= c * chunk) | (t_idx < c * chunk))

        def fast_scores(u):
            def fn(q_ins, ks, bs):
                k_rel = jnp.concatenate([(k * jnp.exp2(-b)).astype(BF16) for k, b in zip(ks, bs)],
                                        axis=0)
                sc = lax.dot_general(jnp.concatenate(q_ins, axis=0), k_rel,
                                     (((1,), (1,)), ((), ())), preferred_element_type=F32)
                return jnp.where(keep, sc, 0.0)
            return fn
        run(fast_scores, lambda u: entering[u])

        @pl.when(jnp.logical_not(safe))
        def _():
            sub_c = lax.broadcasted_iota(jnp.int32, (SUBLANES, chunk), 0)
            col_c = lax.broadcasted_iota(jnp.int32, (SUBLANES, chunk), 1)
            consts = (col_c - sub_c, col_c)

            def exact_scores(u):
                return lambda q_ins, ks, bs: _block_diag(
                    [_hgrn_scores_exact(qkb_ref.at[u + c], consts, chunk=chunk) for c in range(span)])
            run(exact_scores, lambda u: stb_ref[u])
        return carry

    lax.fori_loop(0, n_iter if across_heads else heads_per_step * n_iter, body, 0)

    @pl.when(tt == pl.num_programs(2) - 1)
    def _():
        for hh in range(heads_per_step):
            s_out_ref[0, hh] = st_ref[hh].T


def _hgrn(proj, lbp, ng, s0, *, layer, batch, seq, tc, chunk, heads_per_step):
    m = proj.shape[1]
    nt = seq // tc
    hb = heads_per_step
    n_chunks = tc // chunk
    unroll = min(HGRN_UNROLL, n_chunks) if n_chunks > 1 else min(HGRN_HEADS_UNROLL, hb)

    def sec(cb):
        return pl.BlockSpec((hb, tc, LANES), lambda b, h, t: (cb // hb + h, b * nt + t, 0))

    return pl.pallas_call(
        functools.partial(_hgrn_kernel, layer=layer, chunk=chunk, n_chunks=tc // chunk,
                          heads_per_step=hb, unroll=unroll),
        grid=(batch, HEADS // hb, nt),
        in_specs=[
            pl.BlockSpec((hb,) + lbp.shape[1:], lambda b, h, t: (h, 0, 0)),
            pl.BlockSpec((hb, 1, LANES), lambda b, h, t: (h, 0, 0)),
            sec(CB_Q), sec(CB_F), sec(CB_V), sec(CB_GH),
            pl.BlockSpec((None, 1, hb, HEAD_DIM, HEAD_DIM), lambda b, h, t: (layer, b, h, 0, 0)),
        ],
        out_specs=[
            pl.BlockSpec((hb, tc, LANES), lambda b, h, t: (h, b * nt + t, 0)),
            pl.BlockSpec((1, hb, HEAD_DIM, HEAD_DIM), lambda b, h, t: (b, h, 0, 0)),
        ],
        out_shape=[
            jax.ShapeDtypeStruct((HEADS, m, LANES), BF16),
            jax.ShapeDtypeStruct((batch, HEADS, HEAD_DIM, HEAD_DIM), F32),
        ],
        scratch_shapes=[pltpu.VMEM((hb, HEAD_DIM, HEAD_DIM), F32),
                        pltpu.VMEM((unroll, 3, SUBLANES + chunk, LANES), F32),
                        pltpu.VMEM((unroll if n_chunks == 1 else 1, HEAD_DIM, HEAD_DIM), F32)],
        compiler_params=pltpu.CompilerParams(
            dimension_semantics=("parallel", "parallel", "arbitrary"),
            vmem_limit_bytes=VMEM_LIMIT_BYTES),
        name="hgrn2",
    )(lbp, ng, proj, proj, proj, proj, s0)


def _hgrn_fused_kernel(lbp_ref, ng_ref, x_ref, g_ref, w_ref, s0_ref, y_ref, s_out_ref,
                       u_ref, p_ref, st_ref, qkb_ref, stb_ref, *, layer, chunk, tm, piece):
    i = pl.program_id(1)
    h = pl.program_id(2)
    unroll = tm // chunk
    span = min(HGRN_SPAN, unroll)

    @pl.when(h == 0)
    def _():
        x = x_ref[...]
        ms = jnp.mean(x * x, axis=-1, keepdims=True)
        u_ref[...] = (x * lax.rsqrt(ms + EPS) * g_ref[...]).astype(BF16)

    @pl.when(i == 0)
    def _():
        st_ref[h] = s0_ref[0, h].T

    for u in range(unroll):
        qkb_ref[u, :, 0:SUBLANES, :] = jnp.zeros((3, SUBLANES, LANES), F32)
    sub = lax.broadcasted_iota(jnp.int32, (SUBLANES, LANES), 0)
    shift_ok = [sub >= d for d in (1, 2, 4)]

    for r in range(tm // piece):
        rr = slice(r * piece, (r + 1) * piece)
        p_ref[rr, :] = jnp.dot(u_ref[rr, :], w_ref[0], preferred_element_type=F32)

    def rows(u, n=1):
        return slice(u * chunk, (u + n) * chunk)

    lb = _lower_bound(lbp_ref[h], layer)
    ng = ng_ref[h]
    b_ends = [_hgrn_prep(p_ref[rows(u), 0:LANES], p_ref[rows(u), LANES:2 * LANES], lb, qkb_ref.at[u],
                         shift_ok, chunk=chunk) for u in range(unroll)]
    lowest = functools.reduce(jnp.minimum, b_ends)
    safe = jnp.min(lowest, axis=1, keepdims=True)[0, 0] >= -HGRN_SAFE_LOG2

    def run(scores_fn, st):
        starts = list(range(0, unroll, span))
        runs = [dict(qkb=[qkb_ref.at[u + c] for c in range(span)], b_ends=b_ends[u:u + span],
                     v=p_ref[rows(u, span), 2 * LANES:3 * LANES],
                     g_h=p_ref[rows(u, span), 3 * LANES:4 * LANES],
                     ng=ng, scores_fn=scores_fn(u), st=st if u == 0 else None)
                for u in starts]
        outs, finals = _hgrn_finish(runs)
        for u, y in zip(starts, outs):
            y_ref[0, rows(u, span), :] = y.astype(BF16)
        st_ref[h] = finals[-1]

    entering = st_ref[h]
    stb_ref[0] = entering
    t_idx = lax.broadcasted_iota(jnp.int32, (span * chunk, span * chunk), 0)
    s_idx = lax.broadcasted_iota(jnp.int32, (span * chunk, span * chunk), 1)
    keep = s_idx <= t_idx
    for c in range(1, span):
        keep = keep & ((s_idx >= c * chunk) | (t_idx < c * chunk))

    def fast_scores(u):
        def fn(q_ins, ks, bs):
            k_rel = jnp.concatenate([(k * jnp.exp2(-b)).astype(BF16) for k, b in zip(ks, bs)], axis=0)
            sc = lax.dot_general(jnp.concatenate(q_ins, axis=0), k_rel,
                                 (((1,), (1,)), ((), ())), preferred_element_type=F32)
            return jnp.where(keep, sc, 0.0)
        return fn
    run(fast_scores, entering)

    @pl.when(jnp.logical_not(safe))
    def _():
        sub_c = lax.broadcasted_iota(jnp.int32, (SUBLANES, chunk), 0)
        col_c = lax.broadcasted_iota(jnp.int32, (SUBLANES, chunk), 1)
        consts = (col_c - sub_c, col_c)

        def exact_scores(u):
            return lambda q_ins, ks, bs: _block_diag(
                [_hgrn_scores_exact(qkb_ref.at[u + c], consts, chunk=chunk) for c in range(span)])
        run(exact_scores, stb_ref[0])

    @pl.when(i == pl.num_programs(1) - 1)
    def _():
        s_out_ref[0, h] = st_ref[h].T


def _hgrn_fused(x, g, w_heads, lbp, ng, s0, *, layer, batch, seq, tm, chunk, piece):
    m = x.shape[0]
    nt = seq // tm
    unroll = tm // chunk
    const3 = lambda b, i, h: (0, 0, 0)
    return pl.pallas_call(
        functools.partial(_hgrn_fused_kernel, layer=layer, chunk=chunk, tm=tm, piece=piece),
        grid=(batch, nt, HEADS),
        in_specs=[
            pl.BlockSpec(lbp.shape, const3),
            pl.BlockSpec(ng.shape, const3),
            pl.BlockSpec((tm, D_MODEL), lambda b, i, h: (b * nt + i, 0)),
            pl.BlockSpec((1, D_MODEL), lambda b, i, h: (0, 0)),
            pl.BlockSpec((None, 1, D_MODEL, 4 * LANES), lambda b, i, h: (layer, h, 0, 0)),
            pl.BlockSpec((None, 1, HEADS, HEAD_DIM, HEAD_DIM), lambda b, i, h: (layer, b, 0, 0, 0)),
        ],
        out_specs=[
            pl.BlockSpec((1, tm, LANES), lambda b, i, h: (h, b * nt + i, 0)),
            pl.BlockSpec((1, HEADS, HEAD_DIM, HEAD_DIM), lambda b, i, h: (b, 0, 0, 0)),
        ],
        out_shape=[
            jax.ShapeDtypeStruct((HEADS, m, LANES), BF16),
            jax.ShapeDtypeStruct((batch, HEADS, HEAD_DIM, HEAD_DIM), F32),
        ],
        scratch_shapes=[
            pltpu.VMEM((tm, D_MODEL), BF16),
            pltpu.VMEM((tm, 4 * LANES), F32),
            pltpu.VMEM((HEADS, HEAD_DIM, HEAD_DIM), F32),
            pltpu.VMEM((unroll, 3, SUBLANES + chunk, LANES), F32),
            pltpu.VMEM((1, HEAD_DIM, HEAD_DIM), F32),
        ],
        compiler_params=pltpu.CompilerParams(
            dimension_semantics=("parallel", "arbitrary", "arbitrary"),
            vmem_limit_bytes=VMEM_LIMIT_BYTES),
        name="hgrn2_fused",
    )(lbp, ng, x, g, w_heads, s0)


def _rglru_kernel(xr_ref, gr_ref, hist_ref, h0_ref, cw_ref, cb_ref, wa_ref, ba_ref, wx_ref, bx_ref,
                  lam_ref, y_ref, conv_out_ref, h_out_ref, xbuf_ref, hcar_ref, a_ref, b_ref, p_ref, l_ref,
                  *, tc, blocks_per_step, stream_start, n_vregs):
    tt = pl.program_id(2)
    last = tt == pl.num_programs(2) - 1
    hist_lo = SUBLANES - (CONV_W - 1)
    n_seg = SUBLANES * n_vregs
    seg = tc // n_seg
    pitch = seg + SCAN_PAD
    sub = lax.broadcasted_iota(jnp.int32, (SUBLANES, LANES), 0)

    @pl.when(tt == 0)
    def _():
        for jj in range(blocks_per_step):
            xbuf_ref[jj, hist_lo:SUBLANES, :] = hist_ref[0, jj]
            hcar_ref[jj] = jnp.broadcast_to(h0_ref[0, jj], (SUBLANES, LANES))

    def block_body(jj):
        x = xr_ref[jj]
        xbuf_ref[jj, SUBLANES:SUBLANES + tc, :] = x
        cw = cw_ref[jj]
        xc = cb_ref[jj] + xbuf_ref[jj, pl.ds(hist_lo, tc), :] * cw[0:1, :]
        for j in range(1, CONV_W):
            xc = xc + xbuf_ref[jj, pl.ds(hist_lo + j, tc), :] * cw[j:j + 1, :]
        tail = x[tc - (CONV_W - 1):tc, :]
        xbuf_ref[jj, hist_lo:SUBLANES, :] = tail

        xcb = xc.astype(BF16)
        r = jax.nn.sigmoid(jnp.dot(xcb, wa_ref[jj], preferred_element_type=F32) + ba_ref[jj])
        gi = jax.nn.sigmoid(jnp.dot(xcb, wx_ref[jj], preferred_element_type=F32) + bx_ref[jj])
        nl = -lam_ref[jj]
        rate = -RG_C * (jnp.maximum(nl, 0.0) + jnp.log1p(jnp.exp(-jnp.abs(nl))))
        log_a = r * rate
        a = jnp.exp2(r * (rate * LOG2E))
        m2 = -jnp.tanh(log_a) * (a * a + 1.0)
        mult = jnp.where(m2 > 0.0, m2 * lax.rsqrt(m2), 0.0)
        if stream_start:
            first = jnp.where((sub == 0) & (tt == 0), 1.0, mult[0:SUBLANES])
            mult = jnp.concatenate([first, mult[SUBLANES:]], axis=0)
        bt = mult * gi * xc
        for s in range(n_seg):
            a_ref[jj, s * pitch:s * pitch + seg, :] = a[s * seg:(s + 1) * seg]
            b_ref[jj, s * pitch:s * pitch + seg, :] = bt[s * seg:(s + 1) * seg]

        def scan_body(i, pl_prev):
            out = []
            for v, (p_prev, l_prev) in enumerate(pl_prev):
                rows = pl.ds(i + v * SUBLANES * pitch, SUBLANES, stride=pitch)
                ai = a_ref.at[jj][rows, :]
                l_new = ai * l_prev + b_ref.at[jj][rows, :]
                p_new = ai * p_prev
                p_ref.at[jj][rows, :] = p_new
                l_ref.at[jj][rows, :] = l_new
                out.append((p_new, l_new))
            return tuple(out)

        init = tuple((jnp.ones((SUBLANES, LANES), F32), jnp.zeros((SUBLANES, LANES), F32))
                     for _ in range(n_vregs))
        ends = lax.fori_loop(0, seg, scan_body, init, unroll=min(8, seg))
        h_in = hcar_ref[jj]
        pieces = []
        for v, (p_end, l_end) in enumerate(ends):
            for d in (1, 2, 4):
                keep = sub >= d
                l_end = p_end * jnp.where(keep, pltpu.roll(l_end, d, 0), 0.0) + l_end
                p_end = p_end * jnp.where(keep, pltpu.roll(p_end, d, 0), 1.0)
            h_seg_end = p_end * h_in + l_end
            h_seg_start = jnp.where(sub >= 1, pltpu.roll(h_seg_end, 1, 0), h_in)
            h_in = jnp.broadcast_to(h_seg_end[SUBLANES - 1:SUBLANES, :], (SUBLANES, LANES))
            for r in range(SUBLANES):
                lo = (v * SUBLANES + r) * pitch
                pieces.append(p_ref[jj, lo:lo + seg, :] * h_seg_start[r:r + 1, :]
                              + l_ref[jj, lo:lo + seg, :])
        h_last = h_in
        hcar_ref[jj] = h_last
        h = jnp.concatenate(pieces, axis=0)
        y_ref[jj] = (h * _silu(gr_ref[jj])).astype(BF16)

        @pl.when(last)
        def _():
            conv_out_ref[0, jj] = tail
            h_out_ref[0, jj] = h_last[0:1, :]

    for jj in range(blocks_per_step):
        block_body(jj)


def _rglru(proj, hist, h0, cw, cb, wa, ba, wx, bx, lam, *, batch, seq, tc, blocks_per_step,
           stream_start, cb0):
    m = proj.shape[1]
    nt = seq // tc
    jb = blocks_per_step
    n_vregs = min(SCAN_VREGS, tc // SUBLANES)

    def sec(cb_off):
        return pl.BlockSpec((jb, tc, LANES), lambda b, j, t: (cb_off // jb + j, b * nt + t, 0))

    def per_block(shape):
        return pl.BlockSpec((jb,) + shape, lambda b, j, t: (j,) + (0,) * len(shape))

    return pl.pallas_call(
        functools.partial(_rglru_kernel, tc=tc, blocks_per_step=jb, stream_start=stream_start,
                          n_vregs=n_vregs),
        grid=(batch, RG_BLOCKS // jb, nt),
        in_specs=[
            sec(cb0), sec(cb0 + CB_GR - CB_XR),
            pl.BlockSpec((1, jb, CONV_W - 1, LANES), lambda b, j, t: (b, j, 0, 0)),
            pl.BlockSpec((1, jb, 1, LANES), lambda b, j, t: (b, j, 0, 0)),
            per_block((CONV_W, LANES)), per_block((1, LANES)),
            per_block((LANES, LANES)), per_block((1, LANES)),
            per_block((LANES, LANES)), per_block((1, LANES)),
            per_block((1, LANES)),
        ],
        out_specs=[
            pl.BlockSpec((jb, tc, LANES), lambda b, j, t: (j, b * nt + t, 0)),
            pl.BlockSpec((1, jb, CONV_W - 1, LANES), lambda b, j, t: (b, j, 0, 0)),
            pl.BlockSpec((1, jb, 1, LANES), lambda b, j, t: (b, j, 0, 0)),
        ],
        out_shape=[
            jax.ShapeDtypeStruct((RG_BLOCKS, m, LANES), BF16),
            jax.ShapeDtypeStruct((batch, RG_BLOCKS, CONV_W - 1, LANES), F32),
            jax.ShapeDtypeStruct((batch, RG_BLOCKS, 1, LANES), F32),
        ],
        scratch_shapes=[
            pltpu.VMEM((jb, SUBLANES + tc, LANES), F32),
            pltpu.VMEM((jb, SUBLANES, LANES), F32),
        ] + [pltpu.VMEM((jb, tc + SUBLANES * n_vregs * SCAN_PAD, LANES), F32)] * 4,
        compiler_params=pltpu.CompilerParams(
            dimension_semantics=("parallel", "parallel", "arbitrary"),
            vmem_limit_bytes=VMEM_LIMIT_BYTES),
        name="rglru",
    )(proj, proj, hist, h0, cw, cb, wa, ba, wx, bx, lam)


def _out_kernel(yh_ref, yr_ref, mh_ref, mr_ref, x_ref, wbh_ref, wbr_ref, wo_ref, g_ref, o_ref):
    def gather(ref):
        return jnp.concatenate([ref[c] for c in range(D_MODEL // LANES)], axis=1)

    b_h = jnp.dot(gather(yh_ref), wbh_ref[...], preferred_element_type=F32)
    b_r = jnp.dot(gather(yr_ref), wbr_ref[...], preferred_element_type=F32)
    merged = jax.nn.sigmoid(gather(mh_ref)) * b_h + jax.nn.sigmoid(gather(mr_ref)) * b_r
    out = jnp.dot(merged.astype(BF16), wo_ref[...], preferred_element_type=F32)
    ms = jnp.mean(out * out, axis=-1, keepdims=True)
    o_ref[...] = x_ref[...] + out * lax.rsqrt(ms + EPS) * g_ref[...]


def _out_stage(yh, yr, proj, x, wbh, wbr, wo, g, *, layer, tm, cb0):
    m = x.shape[0]
    nb = D_MODEL // LANES

    def resident(shape):
        return pl.BlockSpec(shape, lambda i: (0,) * len(shape), pipeline_mode=pl.Buffered(1))

    stacked_weight = pl.BlockSpec((None, D_MODEL, D_MODEL), lambda i: (layer, 0, 0),
                                  pipeline_mode=pl.Buffered(1))

    return pl.pallas_call(
        _out_kernel,
        grid=(m // tm,),
        in_specs=[
            pl.BlockSpec((nb, tm, LANES), lambda i: (0, i, 0)),
            pl.BlockSpec((nb, tm, LANES), lambda i: (0, i, 0)),
            pl.BlockSpec((nb, tm, LANES), lambda i: ((cb0 + CB_MH - CB_XR) // nb, i, 0)),
            pl.BlockSpec((nb, tm, LANES), lambda i: ((cb0 + CB_MR - CB_XR) // nb, i, 0)),
            pl.BlockSpec((tm, D_MODEL), lambda i: (i, 0)),
            stacked_weight, stacked_weight, stacked_weight,
            resident((1, D_MODEL)),
        ],
        out_specs=pl.BlockSpec((tm, D_MODEL), lambda i: (i, 0)),
        out_shape=jax.ShapeDtypeStruct((m, D_MODEL), F32),
        compiler_params=pltpu.CompilerParams(
            dimension_semantics=("parallel",), vmem_limit_bytes=VMEM_LIMIT_BYTES),
        name="out_stage",
    )(yh, yr, proj, proj, x, wbh, wbr, wo, g)


def _stream_tiles(batch, seq):
    m = batch * seq
    tm_in = min(m, 1024)
    tn_in = 2048
    tm_out = min(m, 256)
    tc = min(seq, 4096)
    chunk = min(seq, HGRN_CHUNK)
    per_step = 1 if seq >= 1024 else HEADS
    return dict(tm_in=tm_in, tn_in=tn_in, tm_out=tm_out, tc=tc, chunk=chunk, per_step=per_step)


def _run_trunk(x, s_hgrn, conv_hist, h_rg, stream_start, w):
    batch, seq, _ = x.shape
    m = batch * seq
    t = _stream_tiles(batch, seq)
    xf = x.reshape(m, D_MODEL)
    new_s, new_conv, new_h = [], [], []
    for layer in range(w["w_in"].shape[0]):
        pre_g = w["pre_norm_g"][layer]
        if seq % FUSED_TM == 0:
            yh, s_l = _hgrn_fused(xf, pre_g, w["w_heads"], w["lbp"], w["hgrn_norm_g"][layer], s_hgrn,
                                  layer=layer, batch=batch, seq=seq, tm=FUSED_TM, chunk=HGRN_CHUNK,
                                  piece=FUSED_PIECE)
            proj = _in_proj(xf, pre_g, w["w_in"], layer=layer, tm=t["tm_in"], tn=t["tn_in"],
                            col0=4 * D_MODEL, n_cols=4 * D_MODEL)
            cb0 = 0
        else:
            proj = _in_proj(xf, pre_g, w["w_in"], layer=layer, tm=t["tm_in"], tn=t["tn_in"])
            cb0 = CB_XR
            yh, s_l = _hgrn(proj, w["lbp"], w["hgrn_norm_g"][layer], s_hgrn, layer=layer,
                            batch=batch, seq=seq, tc=t["tc"], chunk=t["chunk"],
                            heads_per_step=t["per_step"])
        hist = conv_hist[layer].reshape(batch, CONV_W - 1, RG_BLOCKS, LANES).transpose(0, 2, 1, 3)
        h0 = h_rg[layer].reshape(batch, RG_BLOCKS, 1, LANES)
        yr, conv_l, h_l = _rglru(
            proj, hist, h0, w["conv_w"][layer], w["conv_b"][layer], w["w_gate_a"][layer],
            w["b_gate_a"][layer], w["w_gate_x"][layer], w["b_gate_x"][layer], w["rg_lambda"][layer],
            batch=batch, seq=seq, tc=t["tc"], blocks_per_step=t["per_step"],
            stream_start=stream_start, cb0=cb0)
        xf = _out_stage(yh, yr, proj, xf, w["w_branch_hgrn"], w["w_branch_rglru"], w["w_out"],
                        w["post_norm_g"][layer], layer=layer, tm=t["tm_out"], cb0=cb0)
        new_s.append(s_l)
        new_conv.append(conv_l.transpose(0, 2, 1, 3).reshape(batch, CONV_W - 1, D_MODEL))
        new_h.append(h_l.reshape(batch, D_MODEL))
    return xf.reshape(batch, seq, D_MODEL), jnp.stack(new_s), jnp.stack(new_conv), jnp.stack(new_h)


def _prep_weights(lb_param, pre_norm_g, w_in, hgrn_norm_g, w_branch_hgrn, conv_w, conv_b, w_gate_a,
                  b_gate_a, w_gate_x, b_gate_x, rg_lambda, w_branch_rglru, w_out, post_norm_g):
    depth = w_in.shape[0]

    def rows(a):
        return a.reshape(depth, 1, D_MODEL)

    def blocks(a):
        return a.reshape(depth, RG_BLOCKS, 1, LANES)

    w_heads = w_in[:, :, :4 * D_MODEL].reshape(depth, D_MODEL, 4, HEADS, LANES)
    w_heads = w_heads.transpose(0, 3, 1, 2, 4).reshape(depth, HEADS, D_MODEL, 4 * LANES).astype(BF16)
    return dict(
        lbp=lb_param.astype(F32).reshape(depth, HEADS, LANES).transpose(1, 0, 2),
        pre_norm_g=rows(pre_norm_g), post_norm_g=rows(post_norm_g),
        hgrn_norm_g=hgrn_norm_g.reshape(depth, HEADS, 1, LANES),
        w_in=w_in.astype(BF16), w_heads=w_heads, w_branch_hgrn=w_branch_hgrn.astype(BF16),
        w_branch_rglru=w_branch_rglru.astype(BF16), w_out=w_out.astype(BF16),
        conv_w=conv_w.reshape(depth, CONV_W, RG_BLOCKS, LANES).transpose(0, 2, 1, 3),
        conv_b=blocks(conv_b), w_gate_a=w_gate_a.astype(BF16), b_gate_a=blocks(b_gate_a),
        w_gate_x=w_gate_x.astype(BF16), b_gate_x=blocks(b_gate_x), rg_lambda=blocks(rg_lambda),
    )


def kernel(x_prompt, x_sample, state_hgrn, state_conv, state_rglru, lb_param, pre_norm_g, w_in,
           hgrn_norm_g, w_branch_hgrn, conv_w, conv_b, w_gate_a, b_gate_a, w_gate_x, b_gate_x,
           rg_lambda, w_branch_rglru, w_out, post_norm_g):
    w = _prep_weights(lb_param, pre_norm_g, w_in, hgrn_norm_g, w_branch_hgrn, conv_w, conv_b,
                      w_gate_a, b_gate_a, w_gate_x, b_gate_x, rg_lambda, w_branch_rglru, w_out,
                      post_norm_g)
    depth = w_in.shape[0]
    nb = x_prompt.shape[0]
    s0 = jnp.zeros((depth, nb, HEADS, HEAD_DIM, HEAD_DIM), F32)
    c0 = jnp.zeros((depth, nb, CONV_W - 1, D_MODEL), F32)
    h0 = jnp.zeros((depth, nb, D_MODEL), F32)
    y_p, s_p, c_p, h_p = _run_trunk(x_prompt, s0, c0, h0, True, w)
    y_s, s_s, c_s, h_s = _run_trunk(x_sample, state_hgrn, state_conv, state_rglru, False, w)
    return (y_p, y_s, s_p, c_p, h_p, s_s, c_s, h_s)
```

```python
import functools

import jax
import jax.numpy as jnp
from jax import lax
from jax.experimental import pallas as pl
from jax.experimental.pallas import tpu as pltpu

F32 = jnp.float32
BF16 = jnp.bfloat16

D_MODEL = 2048
HEADS = 16
HEAD_DIM = 128
RG_BLOCKS = 16
CONV_W = 4
RG_C = 8.0
EPS = 1e-6
LANES = 128
SUBLANES = 8
PROJ = 8 * D_MODEL
N_COL_BLOCKS = PROJ // LANES
CB_Q, CB_F, CB_V, CB_GH, CB_XR, CB_GR, CB_MH, CB_MR = (16 * i for i in range(8))
HGRN_CHUNK = 64
HGRN_UNROLL = 32
LOG2E = 1.4426950408889634
HGRN_HEADS_UNROLL = 16
HGRN_SPAN = 2
HGRN_SAFE_LOG2 = 120.0
SCAN_VREGS = 4
SCAN_PAD = 4
VMEM_LIMIT_BYTES = 56 * 1024 * 1024


def _sigmoid_pair(z):
    e = jnp.exp(-jnp.abs(z))
    inv = 1.0 / (1.0 + e)
    pos = z >= 0
    sig_pos = jnp.where(pos, 1.0, e) * inv
    sig_neg = jnp.where(pos, e, 1.0) * inv
    log_sig = jnp.minimum(z, 0.0) - jnp.log1p(e)
    return sig_pos, sig_neg, log_sig


def _silu(x):
    return x * jax.nn.sigmoid(x)


def _in_proj_kernel(x_ref, g_ref, w_ref, o_ref, u_ref, *, tn):
    @pl.when(pl.program_id(1) == 0)
    def _():
        x = x_ref[...]
        ms = jnp.mean(x * x, axis=-1, keepdims=True)
        u_ref[...] = (x * lax.rsqrt(ms + EPS) * g_ref[...]).astype(BF16)

    res = jnp.dot(u_ref[...], w_ref[...], preferred_element_type=F32)
    for c in range(tn // LANES):
        o_ref[c] = res[:, c * LANES:(c + 1) * LANES]


def _in_proj(x, g, w, *, layer, tm, tn):
    m = x.shape[0]
    return pl.pallas_call(
        functools.partial(_in_proj_kernel, tn=tn),
        grid=(m // tm, PROJ // tn),
        in_specs=[
            pl.BlockSpec((tm, D_MODEL), lambda i, j: (i, 0)),
            pl.BlockSpec((1, D_MODEL), lambda i, j: (0, 0)),
            pl.BlockSpec((None, D_MODEL, tn), lambda i, j: (layer, 0, j)),
        ],
        out_specs=pl.BlockSpec((tn // LANES, tm, LANES), lambda i, j: (j, i, 0)),
        out_shape=jax.ShapeDtypeStruct((N_COL_BLOCKS, m, LANES), F32),
        scratch_shapes=[pltpu.VMEM((tm, D_MODEL), BF16)],
        compiler_params=pltpu.CompilerParams(
            dimension_semantics=("parallel", "arbitrary"), vmem_limit_bytes=VMEM_LIMIT_BYTES),
        name="in_proj",
    )(x, g, w)


def _hgrn_prep(q_raw, z, lb, qkb_ref, shift_ok, *, chunk):
    n_groups = chunk // SUBLANES
    q = _silu(q_raw)
    sig_pos, sig_neg, log_sig = _sigmoid_pair(z)
    if lb is None:
        log_f, k = log_sig, sig_neg
    else:
        log_f = jnp.log(lb + (1.0 - lb) * sig_pos)
        k = (1.0 - lb) * sig_neg
    lf2 = log_f * LOG2E
    b_rows = slice(SUBLANES, SUBLANES + chunk)
    scans = []
    for g in range(n_groups):
        bg = lf2[g * SUBLANES:(g + 1) * SUBLANES]
        for ok, d in zip(shift_ok, (1, 2, 4)):
            bg = bg + jnp.where(ok, pltpu.roll(bg, d, 0), 0.0)
        scans.append(bg)
    qkb_ref[0, b_rows, :] = q
    qkb_ref[1, b_rows, :] = k
    qkb_ref[2, b_rows, :] = jnp.concatenate(scans, axis=0)
    run = None
    for g in range(n_groups):
        lo = SUBLANES + g * SUBLANES
        tot = qkb_ref[2, lo + SUBLANES - 1:lo + SUBLANES, :]
        if run is not None:
            qkb_ref[2, lo:lo + SUBLANES, :] = scans[g] + run
            tot = tot + run
        run = tot
    return run


def _hgrn_scores_exact(qkb_ref, consts, *, chunk):
    diag_c, col_c = consts
    n_groups = chunk // SUBLANES

    def grp(plane, g, shift=0):
        lo = SUBLANES + g * SUBLANES - shift
        return qkb_ref[plane, lo:lo + SUBLANES, :]

    def end_row(g):
        lo = SUBLANES + g * SUBLANES + SUBLANES - 1
        return qkb_ref[2, lo:lo + 1, :]

    p_rows = []
    for g in range(n_groups):
        acc = jnp.zeros((SUBLANES, chunk), F32)
        for d in range(SUBLANES):
            a_d = grp(0, g) * grp(1, g, d)
            if d:
                a_d = a_d * jnp.exp2(grp(2, g) - grp(2, g, d))
            acc = jnp.where(diag_c == g * SUBLANES - d, jnp.sum(a_d, axis=-1, keepdims=True), acc)
        p_rows.append(jnp.where(col_c >= g * SUBLANES, acc, 0.0) if g else acc)

    half = chunk // 2
    while half >= SUBLANES:
        gph = half // SUBLANES
        q_parts, k_parts, q_groups = [], [], []
        for g in range(n_groups):
            first = (g // gph) * gph
            if (g // gph) % 2 == 1:
                q_parts.append(grp(0, g) * jnp.exp2(grp(2, g) - end_row(first - 1)))
                q_groups.append(g)
                k_parts.append(jnp.zeros((SUBLANES, LANES), F32))
            else:
                k_parts.append(grp(1, g) * jnp.exp2(end_row(first + gph - 1) - grp(2, g)))
        ql = jnp.concatenate(q_parts, axis=0).astype(BF16)
        kl = jnp.concatenate(k_parts, axis=0).astype(BF16)
        sc = lax.dot_general(ql, kl, (((1,), (1,)), ((), ())), preferred_element_type=F32)
        for j, g in enumerate(q_groups):
            lo = ((g // gph) - 1) * half
            valid = (col_c >= lo) & (col_c < lo + half)
            p_rows[g] = p_rows[g] + jnp.where(valid, sc[j * SUBLANES:(j + 1) * SUBLANES], 0.0)
        half //= 2
    return jnp.concatenate(p_rows, axis=0)


def _block_diag(blocks):
    n = len(blocks)
    zero = jnp.zeros_like(blocks[0])
    return jnp.concatenate(
        [jnp.concatenate([blk if j == i else zero for j in range(n)], axis=1)
         for i, blk in enumerate(blocks)], axis=0)


def _hgrn_finish(runs):
    chunk = runs[0]["v"].shape[0] // len(runs[0]["qkb"])
    rows = slice(SUBLANES, SUBLANES + chunk)
    staged = []
    for r in runs:
        bs = [ref[2, rows, :] for ref in r["qkb"]]
        ks = [ref[1, rows, :] for ref in r["qkb"]]
        q_ins = [(ref[0, rows, :] * jnp.exp2(b)).astype(BF16) for ref, b in zip(r["qkb"], bs)]
        k_outs = [(k * jnp.exp2(e - b)).astype(BF16) for k, b, e in zip(ks, bs, r["b_ends"])]
        upd = jnp.dot(r["v"].T.astype(BF16), _block_diag(k_outs), preferred_element_type=F32)
        p_mat = r["scores_fn"](q_ins, ks, bs).astype(BF16)
        staged.append((q_ins, upd, p_mat))
    states, finals = [], []
    st = None
    for r, (q_ins, upd, p_mat) in zip(runs, staged):
        st = st if r["st"] is None else r["st"]
        run_states = []
        for c, e in enumerate(r["b_ends"]):
            run_states.append(st)
            st = st * jnp.exp2(e) + upd[:, c * LANES:(c + 1) * LANES]
        states.append(jnp.concatenate(run_states, axis=1).astype(BF16))
        finals.append(st)
    outs = []
    for r, (q_ins, upd, p_mat), s_cat in zip(runs, staged, states):
        o = jnp.dot(p_mat, r["v"].astype(BF16), preferred_element_type=F32)
        o = o + lax.dot_general(_block_diag(q_ins), s_cat, (((1,), (1,)), ((), ())),
                                preferred_element_type=F32)
        o = o * lax.rsqrt(jnp.mean(o * o, axis=-1, keepdims=True) + EPS)
        outs.append(o * r["ng"] * _silu(r["g_h"]))
    return outs, finals


def _lower_bound(lbp, layer):
    if layer == 0:
        return None
    ex = jnp.exp(lbp - jnp.max(lbp, axis=0, keepdims=True))
    p = ex / jnp.sum(ex, axis=0, keepdims=True)
    return jnp.sum(p[1:layer + 1], axis=0, keepdims=True)


def _hgrn_kernel(lbp_ref, ng_ref, q_ref, f_ref, v_ref, g_ref, s0_ref, y_ref, s_out_ref, st_ref, qkb_ref,
                 stb_ref,
                 *, layer, chunk, n_chunks, heads_per_step, unroll):
    tt = pl.program_id(2)

    @pl.when(tt == 0)
    def _():
        for hh in range(heads_per_step):
            st_ref[hh] = s0_ref[0, hh].T

    for u in range(unroll):
        qkb_ref[u, :, 0:SUBLANES, :] = jnp.zeros((3, SUBLANES, LANES), F32)
    sub = lax.broadcasted_iota(jnp.int32, (SUBLANES, LANES), 0)
    shift_ok = [sub >= d for d in (1, 2, 4)]
    across_heads = n_chunks == 1
    span = 1 if across_heads else min(HGRN_SPAN, unroll)
    n_iter = (heads_per_step if across_heads else n_chunks) // unroll

    def body(idx, carry):
        if across_heads:
            heads = [idx * unroll + u for u in range(unroll)]
            base = 0
        else:
            heads = [idx // n_iter] * unroll
            base = (idx % n_iter) * (unroll * chunk)

        def rows(u, n=1):
            start = 0 if across_heads else pl.multiple_of(base + u * chunk, chunk)
            return pl.ds(start, n * chunk)

        b_ends = [_hgrn_prep(q_ref[heads[u], rows(u), :], f_ref[heads[u], rows(u), :],
                             _lower_bound(lbp_ref[heads[u]], layer), qkb_ref.at[u], shift_ok,
                             chunk=chunk) for u in range(unroll)]
        lowest = functools.reduce(jnp.minimum, b_ends)
        safe = jnp.min(lowest, axis=1, keepdims=True)[0, 0] >= -HGRN_SAFE_LOG2

        def run(scores_fn, state_of):
            starts = list(range(0, unroll, span))
            runs = [dict(qkb=[qkb_ref.at[u + c] for c in range(span)], b_ends=b_ends[u:u + span],
                         v=v_ref[heads[u], rows(u, span), :], g_h=g_ref[heads[u], rows(u, span), :],
                         ng=ng_ref[heads[u]], scores_fn=scores_fn(u),
                         st=state_of(u) if (across_heads or u == 0) else None)
                    for u in starts]
            outs, finals = _hgrn_finish(runs)
            for u, y in zip(starts, outs):
                y_ref[heads[u], rows(u, span), :] = y.astype(BF16)
            if across_heads:
                for u, st in zip(starts, finals):
                    st_ref[heads[u]] = st
            else:
                st_ref[heads[0]] = finals[-1]

        n_saved = unroll if across_heads else 1
        entering = [st_ref[heads[u]] for u in range(n_saved)]
        for u in range(n_saved):
            stb_ref[u] = entering[u]
        t_idx = lax.broadcasted_iota(jnp.int32, (span * chunk, span * chunk), 0)
        s_idx = lax.broadcasted_iota(jnp.int32, (span * chunk, span * chunk), 1)
        keep = s_idx <= t_idx
        for c in range(1, span):
            keep = keep & ((s_idx >= c * chunk) | (t_idx < c * chunk))

        def fast_scores(u):
            def fn(q_ins, ks, bs):
                k_rel = jnp.concatenate([(k * jnp.exp2(-b)).astype(BF16) for k, b in zip(ks, bs)],
                                        axis=0)
                sc = lax.dot_general(jnp.concatenate(q_ins, axis=0), k_rel,
                                     (((1,), (1,)), ((), ())), preferred_element_type=F32)
                return jnp.where(keep, sc, 0.0)
            return fn
        run(fast_scores, lambda u: entering[u])

        @pl.when(jnp.logical_not(safe))
        def _():
            sub_c = lax.broadcasted_iota(jnp.int32, (SUBLANES, chunk), 0)
            col_c = lax.broadcasted_iota(jnp.int32, (SUBLANES, chunk), 1)
            consts = (col_c - sub_c, col_c)

            def exact_scores(u):
                return lambda q_ins, ks, bs: _block_diag(
                    [_hgrn_scores_exact(qkb_ref.at[u + c], consts, chunk=chunk) for c in range(span)])
            run(exact_scores, lambda u: stb_ref[u])
        return carry

    lax.fori_loop(0, n_iter if across_heads else heads_per_step * n_iter, body, 0)

    @pl.when(tt == pl.num_programs(2) - 1)
    def _():
        for hh in range(heads_per_step):
            s_out_ref[0, hh] = st_ref[hh].T


def _hgrn(proj, lbp, ng, s0, *, layer, batch, seq, tc, chunk, heads_per_step):
    m = proj.shape[1]
    nt = seq // tc
    hb = heads_per_step
    n_chunks = tc // chunk
    unroll = min(HGRN_UNROLL, n_chunks) if n_chunks > 1 else min(HGRN_HEADS_UNROLL, hb)

    def sec(cb):
        return pl.BlockSpec((hb, tc, LANES), lambda b, h, t: (cb // hb + h, b * nt + t, 0))

    return pl.pallas_call(
        functools.partial(_hgrn_kernel, layer=layer, chunk=chunk, n_chunks=tc // chunk,
                          heads_per_step=hb, unroll=unroll),
        grid=(batch, HEADS // hb, nt),
        in_specs=[
            pl.BlockSpec((hb,) + lbp.shape[1:], lambda b, h, t: (h, 0, 0)),
            pl.BlockSpec((hb, 1, LANES), lambda b, h, t: (h, 0, 0)),
            sec(CB_Q), sec(CB_F), sec(CB_V), sec(CB_GH),
            pl.BlockSpec((None, 1, hb, HEAD_DIM, HEAD_DIM), lambda b, h, t: (layer, b, h, 0, 0)),
        ],
        out_specs=[
            pl.BlockSpec((hb, tc, LANES), lambda b, h, t: (h, b * nt + t, 0)),
            pl.BlockSpec((1, hb, HEAD_DIM, HEAD_DIM), lambda b, h, t: (b, h, 0, 0)),
        ],
        out_shape=[
            jax.ShapeDtypeStruct((HEADS, m, LANES), BF16),
            jax.ShapeDtypeStruct((batch, HEADS, HEAD_DIM, HEAD_DIM), F32),
        ],
        scratch_shapes=[pltpu.VMEM((hb, HEAD_DIM, HEAD_DIM), F32),
                        pltpu.VMEM((unroll, 3, SUBLANES + chunk, LANES), F32),
                        pltpu.VMEM((unroll if n_chunks == 1 else 1, HEAD_DIM, HEAD_DIM), F32)],
        compiler_params=pltpu.CompilerParams(
            dimension_semantics=("parallel", "parallel", "arbitrary"),
            vmem_limit_bytes=VMEM_LIMIT_BYTES),
        name="hgrn2",
    )(lbp, ng, proj, proj, proj, proj, s0)


def _rglru_kernel(xr_ref, gr_ref, hist_ref, h0_ref, cw_ref, cb_ref, wa_ref, ba_ref, wx_ref, bx_ref,
                  lam_ref, y_ref, conv_out_ref, h_out_ref, xbuf_ref, hcar_ref, a_ref, b_ref, p_ref, l_ref,
                  *, tc, blocks_per_step, stream_start, n_vregs):
    tt = pl.program_id(2)
    last = tt == pl.num_programs(2) - 1
    hist_lo = SUBLANES - (CONV_W - 1)
    n_seg = SUBLANES * n_vregs
    seg = tc // n_seg
    pitch = seg + SCAN_PAD
    sub = lax.broadcasted_iota(jnp.int32, (SUBLANES, LANES), 0)

    @pl.when(tt == 0)
    def _():
        for jj in range(blocks_per_step):
            xbuf_ref[jj, hist_lo:SUBLANES, :] = hist_ref[0, jj]
            hcar_ref[jj] = jnp.broadcast_to(h0_ref[0, jj], (SUBLANES, LANES))

    def block_body(jj):
        x = xr_ref[jj]
        xbuf_ref[jj, SUBLANES:SUBLANES + tc, :] = x
        cw = cw_ref[jj]
        xc = cb_ref[jj] + xbuf_ref[jj, pl.ds(hist_lo, tc), :] * cw[0:1, :]
        for j in range(1, CONV_W):
            xc = xc + xbuf_ref[jj, pl.ds(hist_lo + j, tc), :] * cw[j:j + 1, :]
        tail = x[tc - (CONV_W - 1):tc, :]
        xbuf_ref[jj, hist_lo:SUBLANES, :] = tail

        xcb = xc.astype(BF16)
        r = jax.nn.sigmoid(jnp.dot(xcb, wa_ref[jj], preferred_element_type=F32) + ba_ref[jj])
        gi = jax.nn.sigmoid(jnp.dot(xcb, wx_ref[jj], preferred_element_type=F32) + bx_ref[jj])
        nl = -lam_ref[jj]
        rate = -RG_C * (jnp.maximum(nl, 0.0) + jnp.log1p(jnp.exp(-jnp.abs(nl))))
        log_a = r * rate
        a = jnp.exp2(r * (rate * LOG2E))
        m2 = -jnp.tanh(log_a) * (a * a + 1.0)
        mult = jnp.where(m2 > 0.0, m2 * lax.rsqrt(m2), 0.0)
        if stream_start:
            first = jnp.where((sub == 0) & (tt == 0), 1.0, mult[0:SUBLANES])
            mult = jnp.concatenate([first, mult[SUBLANES:]], axis=0)
        bt = mult * gi * xc
        for s in range(n_seg):
            a_ref[jj, s * pitch:s * pitch + seg, :] = a[s * seg:(s + 1) * seg]
            b_ref[jj, s * pitch:s * pitch + seg, :] = bt[s * seg:(s + 1) * seg]

        def scan_body(i, pl_prev):
            out = []
            for v, (p_prev, l_prev) in enumerate(pl_prev):
                rows = pl.ds(i + v * SUBLANES * pitch, SUBLANES, stride=pitch)
                ai = a_ref.at[jj][rows, :]
                l_new = ai * l_prev + b_ref.at[jj][rows, :]
                p_new = ai * p_prev
                p_ref.at[jj][rows, :] = p_new
                l_ref.at[jj][rows, :] = l_new
                out.append((p_new, l_new))
            return tuple(out)

        init = tuple((jnp.ones((SUBLANES, LANES), F32), jnp.zeros((SUBLANES, LANES), F32))
                     for _ in range(n_vregs))
        ends = lax.fori_loop(0, seg, scan_body, init, unroll=min(8, seg))
        h_in = hcar_ref[jj]
        pieces = []
        for v, (p_end, l_end) in enumerate(ends):
            for d in (1, 2, 4):
                keep = sub >= d
                l_end = p_end * jnp.where(keep, pltpu.roll(l_end, d, 0), 0.0) + l_end
                p_end = p_end * jnp.where(keep, pltpu.roll(p_end, d, 0), 1.0)
            h_seg_end = p_end * h_in + l_end
            h_seg_start = jnp.where(sub >= 1, pltpu.roll(h_seg_end, 1, 0), h_in)
            h_in = jnp.broadcast_to(h_seg_end[SUBLANES - 1:SUBLANES, :], (SUBLANES, LANES))
            for r in range(SUBLANES):
                lo = (v * SUBLANES + r) * pitch
                pieces.append(p_ref[jj, lo:lo + seg, :] * h_seg_start[r:r + 1, :]
                              + l_ref[jj, lo:lo + seg, :])
        h_last = h_in
        hcar_ref[jj] = h_last
        h = jnp.concatenate(pieces, axis=0)
        y_ref[jj] = (h * _silu(gr_ref[jj])).astype(BF16)

        @pl.when(last)
        def _():
            conv_out_ref[0, jj] = tail
            h_out_ref[0, jj] = h_last[0:1, :]

    for jj in range(blocks_per_step):
        block_body(jj)


def _rglru(proj, hist, h0, cw, cb, wa, ba, wx, bx, lam, *, batch, seq, tc, blocks_per_step,
           stream_start):
    m = proj.shape[1]
    nt = seq // tc
    jb = blocks_per_step
    n_vregs = min(SCAN_VREGS, tc // SUBLANES)

    def sec(cb_off):
        return pl.BlockSpec((jb, tc, LANES), lambda b, j, t: (cb_off // jb + j, b * nt + t, 0))

    def per_block(shape):
        return pl.BlockSpec((jb,) + shape, lambda b, j, t: (j,) + (0,) * len(shape))

    return pl.pallas_call(
        functools.partial(_rglru_kernel, tc=tc, blocks_per_step=jb, stream_start=stream_start,
                          n_vregs=n_vregs),
        grid=(batch, RG_BLOCKS // jb, nt),
        in_specs=[
            sec(CB_XR), sec(CB_GR),
            pl.BlockSpec((1, jb, CONV_W - 1, LANES), lambda b, j, t: (b, j, 0, 0)),
            pl.BlockSpec((1, jb, 1, LANES), lambda b, j, t: (b, j, 0, 0)),
            per_block((CONV_W, LANES)), per_block((1, LANES)),
            per_block((LANES, LANES)), per_block((1, LANES)),
            per_block((LANES, LANES)), per_block((1, LANES)),
            per_block((1, LANES)),
        ],
        out_specs=[
            pl.BlockSpec((jb, tc, LANES), lambda b, j, t: (j, b * nt + t, 0)),
            pl.BlockSpec((1, jb, CONV_W - 1, LANES), lambda b, j, t: (b, j, 0, 0)),
            pl.BlockSpec((1, jb, 1, LANES), lambda b, j, t: (b, j, 0, 0)),
        ],
        out_shape=[
            jax.ShapeDtypeStruct((RG_BLOCKS, m, LANES), BF16),
            jax.ShapeDtypeStruct((batch, RG_BLOCKS, CONV_W - 1, LANES), F32),
            jax.ShapeDtypeStruct((batch, RG_BLOCKS, 1, LANES), F32),
        ],
        scratch_shapes=[
            pltpu.VMEM((jb, SUBLANES + tc, LANES), F32),
            pltpu.VMEM((jb, SUBLANES, LANES), F32),
        ] + [pltpu.VMEM((jb, tc + SUBLANES * n_vregs * SCAN_PAD, LANES), F32)] * 4,
        compiler_params=pltpu.CompilerParams(
            dimension_semantics=("parallel", "parallel", "arbitrary"),
            vmem_limit_bytes=VMEM_LIMIT_BYTES),
        name="rglru",
    )(proj, proj, hist, h0, cw, cb, wa, ba, wx, bx, lam)


def _out_kernel(yh_ref, yr_ref, mh_ref, mr_ref, x_ref, wbh_ref, wbr_ref, wo_ref, g_ref, o_ref):
    def gather(ref):
        return jnp.concatenate([ref[c] for c in range(D_MODEL // LANES)], axis=1)

    b_h = jnp.dot(gather(yh_ref), wbh_ref[...], preferred_element_type=F32)
    b_r = jnp.dot(gather(yr_ref), wbr_ref[...], preferred_element_type=F32)
    merged = jax.nn.sigmoid(gather(mh_ref)) * b_h + jax.nn.sigmoid(gather(mr_ref)) * b_r
    out = jnp.dot(merged.astype(BF16), wo_ref[...], preferred_element_type=F32)
    ms = jnp.mean(out * out, axis=-1, keepdims=True)
    o_ref[...] = x_ref[...] + out * lax.rsqrt(ms + EPS) * g_ref[...]


def _out_stage(yh, yr, proj, x, wbh, wbr, wo, g, *, layer, tm):
    m = x.shape[0]
    nb = D_MODEL // LANES

    def resident(shape):
        return pl.BlockSpec(shape, lambda i: (0,) * len(shape), pipeline_mode=pl.Buffered(1))

    stacked_weight = pl.BlockSpec((None, D_MODEL, D_MODEL), lambda i: (layer, 0, 0),
                                  pipeline_mode=pl.Buffered(1))

    return pl.pallas_call(
        _out_kernel,
        grid=(m // tm,),
        in_specs=[
            pl.BlockSpec((nb, tm, LANES), lambda i: (0, i, 0)),
            pl.BlockSpec((nb, tm, LANES), lambda i: (0, i, 0)),
            pl.BlockSpec((nb, tm, LANES), lambda i: (CB_MH // nb, i, 0)),
            pl.BlockSpec((nb, tm, LANES), lambda i: (CB_MR // nb, i, 0)),
            pl.BlockSpec((tm, D_MODEL), lambda i: (i, 0)),
            stacked_weight, stacked_weight, stacked_weight,
            resident((1, D_MODEL)),
        ],
        out_specs=pl.BlockSpec((tm, D_MODEL), lambda i: (i, 0)),
        out_shape=jax.ShapeDtypeStruct((m, D_MODEL), F32),
        compiler_params=pltpu.CompilerParams(
            dimension_semantics=("parallel",), vmem_limit_bytes=VMEM_LIMIT_BYTES),
        name="out_stage",
    )(yh, yr, proj, proj, x, wbh, wbr, wo, g)


def _stream_tiles(batch, seq):
    m = batch * seq
    tm_in = min(m, 1024)
    tn_in = 2048
    tm_out = min(m, 256)
    tc = min(seq, 8192)
    chunk = min(seq, HGRN_CHUNK)
    per_step = 1 if seq >= 1024 else HEADS
    return dict(tm_in=tm_in, tn_in=tn_in, tm_out=tm_out, tc=tc, chunk=chunk, per_step=per_step)


def _run_trunk(x, s_hgrn, conv_hist, h_rg, stream_start, w):
    batch, seq, _ = x.shape
    m = batch * seq
    t = _stream_tiles(batch, seq)
    xf = x.reshape(m, D_MODEL)
    new_s, new_conv, new_h = [], [], []
    for layer in range(w["w_in"].shape[0]):
        proj = _in_proj(xf, w["pre_norm_g"][layer], w["w_in"], layer=layer, tm=t["tm_in"], tn=t["tn_in"])
        yh, s_l = _hgrn(proj, w["lbp"], w["hgrn_norm_g"][layer], s_hgrn, layer=layer,
                        batch=batch, seq=seq, tc=t["tc"], chunk=t["chunk"],
                        heads_per_step=t["per_step"])
        hist = conv_hist[layer].reshape(batch, CONV_W - 1, RG_BLOCKS, LANES).transpose(0, 2, 1, 3)
        h0 = h_rg[layer].reshape(batch, RG_BLOCKS, 1, LANES)
        yr, conv_l, h_l = _rglru(
            proj, hist, h0, w["conv_w"][layer], w["conv_b"][layer], w["w_gate_a"][layer],
            w["b_gate_a"][layer], w["w_gate_x"][layer], w["b_gate_x"][layer], w["rg_lambda"][layer],
            batch=batch, seq=seq, tc=t["tc"], blocks_per_step=t["per_step"],
            stream_start=stream_start)
        xf = _out_stage(yh, yr, proj, xf, w["w_branch_hgrn"], w["w_branch_rglru"], w["w_out"],
                        w["post_norm_g"][layer], layer=layer, tm=t["tm_out"])
        new_s.append(s_l)
        new_conv.append(conv_l.transpose(0, 2, 1, 3).reshape(batch, CONV_W - 1, D_MODEL))
        new_h.append(h_l.reshape(batch, D_MODEL))
    return xf.reshape(batch, seq, D_MODEL), jnp.stack(new_s), jnp.stack(new_conv), jnp.stack(new_h)


def _prep_weights(lb_param, pre_norm_g, w_in, hgrn_norm_g, w_branch_hgrn, conv_w, conv_b, w_gate_a,
                  b_gate_a, w_gate_x, b_gate_x, rg_lambda, w_branch_rglru, w_out, post_norm_g):
    depth = w_in.shape[0]

    def rows(a):
        return a.reshape(depth, 1, D_MODEL)

    def blocks(a):
        return a.reshape(depth, RG_BLOCKS, 1, LANES)

    return dict(
        lbp=lb_param.astype(F32).reshape(depth, HEADS, LANES).transpose(1, 0, 2),
        pre_norm_g=rows(pre_norm_g), post_norm_g=rows(post_norm_g),
        hgrn_norm_g=hgrn_norm_g.reshape(depth, HEADS, 1, LANES),
        w_in=w_in.astype(BF16), w_branch_hgrn=w_branch_hgrn.astype(BF16),
        w_branch_rglru=w_branch_rglru.astype(BF16), w_out=w_out.astype(BF16),
        conv_w=conv_w.reshape(depth, CONV_W, RG_BLOCKS, LANES).transpose(0, 2, 1, 3),
        conv_b=blocks(conv_b), w_gate_a=w_gate_a.astype(BF16), b_gate_a=blocks(b_gate_a),
        w_gate_x=w_gate_x.astype(BF16), b_gate_x=blocks(b_gate_x), rg_lambda=blocks(rg_lambda),
    )


def kernel(x_prompt, x_sample, state_hgrn, state_conv, state_rglru, lb_param, pre_norm_g, w_in,
           hgrn_norm_g, w_branch_hgrn, conv_w, conv_b, w_gate_a, b_gate_a, w_gate_x, b_gate_x,
           rg_lambda, w_branch_rglru, w_out, post_norm_g):
    w = _prep_weights(lb_param, pre_norm_g, w_in, hgrn_norm_g, w_branch_hgrn, conv_w, conv_b,
                      w_gate_a, b_gate_a, w_gate_x, b_gate_x, rg_lambda, w_branch_rglru, w_out,
                      post_norm_g)
    depth = w_in.shape[0]
    nb = x_prompt.shape[0]
    s0 = jnp.zeros((depth, nb, HEADS, HEAD_DIM, HEAD_DIM), F32)
    c0 = jnp.zeros((depth, nb, CONV_W - 1, D_MODEL), F32)
    h0 = jnp.zeros((depth, nb, D_MODEL), F32)
    y_p, s_p, c_p, h_p = _run_trunk(x_prompt, s0, c0, h0, True, w)
    y_s, s_s, c_s, h_s = _run_trunk(x_sample, state_hgrn, state_conv, state_rglru, False, w)
    return (y_p, y_s, s_p, c_p, h_p, s_s, c_s, h_s)
```

```python
import functools

import jax
import jax.numpy as jnp
from jax import lax
from jax.experimental import pallas as pl
from jax.experimental.pallas import tpu as pltpu

F32 = jnp.float32
BF16 = jnp.bfloat16

D_MODEL = 2048
HEADS = 16
HEAD_DIM = 128
RG_BLOCKS = 16
CONV_W = 4
RG_C = 8.0
EPS = 1e-6
LANES = 128
SUBLANES = 8
PROJ = 8 * D_MODEL
N_COL_BLOCKS = PROJ // LANES
CB_Q, CB_F, CB_V, CB_GH, CB_XR, CB_GR, CB_MH, CB_MR = (16 * i for i in range(8))
HGRN_CHUNK = 64
HGRN_UNROLL = 32
LOG2E = 1.4426950408889634
HGRN_HEADS_UNROLL = 16
HGRN_SPAN = 2
HGRN_SAFE_LOG2 = 120.0
SCAN_VREGS = 4
SCAN_PAD = 4
VMEM_LIMIT_BYTES = 56 * 1024 * 1024
IN_PROJ_VMEM_LIMIT_BYTES = 60 * 1024 * 1024


def _sigmoid_pair(z):
    e = jnp.exp(-jnp.abs(z))
    inv = 1.0 / (1.0 + e)
    pos = z >= 0
    sig_pos = jnp.where(pos, 1.0, e) * inv
    sig_neg = jnp.where(pos, e, 1.0) * inv
    log_sig = jnp.minimum(z, 0.0) - jnp.log1p(e)
    return sig_pos, sig_neg, log_sig


def _silu(x):
    return x * jax.nn.sigmoid(x)


def _in_proj_kernel(x_ref, g_ref, w_ref, o_ref, u_ref, r_ref, *, tn):
    @pl.when(pl.program_id(1) == 0)
    def _():
        x = x_ref[...]
        ms = jnp.mean(x * x, axis=-1, keepdims=True)
        r_ref[...] = jnp.broadcast_to(lax.rsqrt(ms + EPS), r_ref.shape)
        u_ref[...] = (x * g_ref[...]).astype(BF16)

    res = jnp.dot(u_ref[...], w_ref[...], preferred_element_type=F32)
    r = r_ref[...]
    for c in range(tn // LANES):
        o_ref[c] = res[:, c * LANES:(c + 1) * LANES] * r


def _in_proj(x, g, w, *, layer, tm, tn):
    m = x.shape[0]
    return pl.pallas_call(
        functools.partial(_in_proj_kernel, tn=tn),
        grid=(m // tm, PROJ // tn),
        in_specs=[
            pl.BlockSpec((tm, D_MODEL), lambda i, j: (i, 0)),
            pl.BlockSpec((1, D_MODEL), lambda i, j: (0, 0)),
            pl.BlockSpec((None, D_MODEL, tn), lambda i, j: (layer, 0, j)),
        ],
        out_specs=pl.BlockSpec((tn // LANES, tm, LANES), lambda i, j: (j, i, 0)),
        out_shape=jax.ShapeDtypeStruct((N_COL_BLOCKS, m, LANES), F32),
        scratch_shapes=[pltpu.VMEM((tm, D_MODEL), BF16), pltpu.VMEM((tm, LANES), F32)],
        compiler_params=pltpu.CompilerParams(
            dimension_semantics=("parallel", "arbitrary"), vmem_limit_bytes=IN_PROJ_VMEM_LIMIT_BYTES),
        name="in_proj",
    )(x, g, w)


def _hgrn_prep(q_raw, z, lb, qkb_ref, shift_ok, *, chunk):
    n_groups = chunk // SUBLANES
    q = _silu(q_raw)
    sig_pos, sig_neg, log_sig = _sigmoid_pair(z)
    if lb is None:
        log_f, k = log_sig, sig_neg
    else:
        log_f = jnp.log(lb + (1.0 - lb) * sig_pos)
        k = (1.0 - lb) * sig_neg
    lf2 = log_f * LOG2E
    b_rows = slice(SUBLANES, SUBLANES + chunk)
    scans = []
    for g in range(n_groups):
        bg = lf2[g * SUBLANES:(g + 1) * SUBLANES]
        for ok, d in zip(shift_ok, (1, 2, 4)):
            bg = bg + jnp.where(ok, pltpu.roll(bg, d, 0), 0.0)
        scans.append(bg)
    qkb_ref[0, b_rows, :] = q
    qkb_ref[1, b_rows, :] = k
    qkb_ref[2, b_rows, :] = jnp.concatenate(scans, axis=0)
    run = None
    for g in range(n_groups):
        lo = SUBLANES + g * SUBLANES
        tot = qkb_ref[2, lo + SUBLANES - 1:lo + SUBLANES, :]
        if run is not None:
            qkb_ref[2, lo:lo + SUBLANES, :] = scans[g] + run
            tot = tot + run
        run = tot
    return run


def _hgrn_scores_exact(qkb_ref, consts, *, chunk):
    diag_c, col_c = consts
    n_groups = chunk // SUBLANES

    def grp(plane, g, shift=0):
        lo = SUBLANES + g * SUBLANES - shift
        return qkb_ref[plane, lo:lo + SUBLANES, :]

    def end_row(g):
        lo = SUBLANES + g * SUBLANES + SUBLANES - 1
        return qkb_ref[2, lo:lo + 1, :]

    p_rows = []
    for g in range(n_groups):
        acc = jnp.zeros((SUBLANES, chunk), F32)
        for d in range(SUBLANES):
            a_d = grp(0, g) * grp(1, g, d)
            if d:
                a_d = a_d * jnp.exp2(grp(2, g) - grp(2, g, d))
            acc = jnp.where(diag_c == g * SUBLANES - d, jnp.sum(a_d, axis=-1, keepdims=True), acc)
        p_rows.append(jnp.where(col_c >= g * SUBLANES, acc, 0.0) if g else acc)

    half = chunk // 2
    while half >= SUBLANES:
        gph = half // SUBLANES
        q_parts, k_parts, q_groups = [], [], []
        for g in range(n_groups):
            first = (g // gph) * gph
            if (g // gph) % 2 == 1:
                q_parts.append(grp(0, g) * jnp.exp2(grp(2, g) - end_row(first - 1)))
                q_groups.append(g)
                k_parts.append(jnp.zeros((SUBLANES, LANES), F32))
            else:
                k_parts.append(grp(1, g) * jnp.exp2(end_row(first + gph - 1) - grp(2, g)))
        ql = jnp.concatenate(q_parts, axis=0).astype(BF16)
        kl = jnp.concatenate(k_parts, axis=0).astype(BF16)
        sc = lax.dot_general(ql, kl, (((1,), (1,)), ((), ())), preferred_element_type=F32)
        for j, g in enumerate(q_groups):
            lo = ((g // gph) - 1) * half
            valid = (col_c >= lo) & (col_c < lo + half)
            p_rows[g] = p_rows[g] + jnp.where(valid, sc[j * SUBLANES:(j + 1) * SUBLANES], 0.0)
        half //= 2
    return jnp.concatenate(p_rows, axis=0)


def _block_diag(blocks):
    n = len(blocks)
    zero = jnp.zeros_like(blocks[0])
    return jnp.concatenate(
        [jnp.concatenate([blk if j == i else zero for j in range(n)], axis=1)
         for i, blk in enumerate(blocks)], axis=0)


def _hgrn_finish(runs):
    chunk = runs[0]["v"].shape[0] // len(runs[0]["qkb"])
    rows = slice(SUBLANES, SUBLANES + chunk)
    staged = []
    for r in runs:
        bs = [ref[2, rows, :] for ref in r["qkb"]]
        ks = [ref[1, rows, :] for ref in r["qkb"]]
        q_ins = [(ref[0, rows, :] * jnp.exp2(b)).astype(BF16) for ref, b in zip(r["qkb"], bs)]
        k_outs = [(k * jnp.exp2(e - b)).astype(BF16) for k, b, e in zip(ks, bs, r["b_ends"])]
        upd = jnp.dot(r["v"].T.astype(BF16), _block_diag(k_outs), preferred_element_type=F32)
        p_mat = r["scores_fn"](q_ins, ks, bs).astype(BF16)
        staged.append((q_ins, upd, p_mat))
    states, finals = [], []
    st = None
    for r, (q_ins, upd, p_mat) in zip(runs, staged):
        st = st if r["st"] is None else r["st"]
        run_states = []
        for c, e in enumerate(r["b_ends"]):
            run_states.append(st)
            st = st * jnp.exp2(e) + upd[:, c * LANES:(c + 1) * LANES]
        states.append(jnp.concatenate(run_states, axis=1).astype(BF16))
        finals.append(st)
    outs = []
    for r, (q_ins, upd, p_mat), s_cat in zip(runs, staged, states):
        o = jnp.dot(p_mat, r["v"].astype(BF16), preferred_element_type=F32)
        o = o + lax.dot_general(_block_diag(q_ins), s_cat, (((1,), (1,)), ((), ())),
                                preferred_element_type=F32)
        o = o * lax.rsqrt(jnp.mean(o * o, axis=-1, keepdims=True) + EPS)
        outs.append(o * r["ng"] * _silu(r["g_h"]))
    return outs, finals


def _lower_bound(lbp, layer):
    if layer == 0:
        return None
    ex = jnp.exp(lbp - jnp.max(lbp, axis=0, keepdims=True))
    p = ex / jnp.sum(ex, axis=0, keepdims=True)
    return jnp.sum(p[1:layer + 1], axis=0, keepdims=True)


def _hgrn_kernel(lbp_ref, ng_ref, q_ref, f_ref, v_ref, g_ref, s0_ref, y_ref, s_out_ref, st_ref, qkb_ref,
                 stb_ref,
                 *, layer, chunk, n_chunks, heads_per_step, unroll):
    tt = pl.program_id(2)

    @pl.when(tt == 0)
    def _():
        for hh in range(heads_per_step):
            st_ref[hh] = s0_ref[0, hh].T

    for u in range(unroll):
        qkb_ref[u, :, 0:SUBLANES, :] = jnp.zeros((3, SUBLANES, LANES), F32)
    sub = lax.broadcasted_iota(jnp.int32, (SUBLANES, LANES), 0)
    shift_ok = [sub >= d for d in (1, 2, 4)]
    across_heads = n_chunks == 1
    span = 1 if across_heads else min(HGRN_SPAN, unroll)
    n_iter = (heads_per_step if across_heads else n_chunks) // unroll

    def body(idx, carry):
        if across_heads:
            heads = [idx * unroll + u for u in range(unroll)]
            base = 0
        else:
            heads = [idx // n_iter] * unroll
            base = (idx % n_iter) * (unroll * chunk)

        def rows(u, n=1):
            start = 0 if across_heads else pl.multiple_of(base + u * chunk, chunk)
            return pl.ds(start, n * chunk)

        b_ends = [_hgrn_prep(q_ref[heads[u], rows(u), :], f_ref[heads[u], rows(u), :],
                             _lower_bound(lbp_ref[heads[u]], layer), qkb_ref.at[u], shift_ok,
                             chunk=chunk) for u in range(unroll)]
        lowest = functools.reduce(jnp.minimum, b_ends)
        safe = jnp.min(lowest, axis=1, keepdims=True)[0, 0] >= -HGRN_SAFE_LOG2

        def run(scores_fn, state_of):
            starts = list(range(0, unroll, span))
            runs = [dict(qkb=[qkb_ref.at[u + c] for c in range(span)], b_ends=b_ends[u:u + span],
                         v=v_ref[heads[u], rows(u, span), :], g_h=g_ref[heads[u], rows(u, span), :],
                         ng=ng_ref[heads[u]], scores_fn=scores_fn(u),
                         st=state_of(u) if (across_heads or u == 0) else None)
                    for u in starts]
            outs, finals = _hgrn_finish(runs)
            for u, y in zip(starts, outs):
                y_ref[heads[u], rows(u, span), :] = y.astype(BF16)
            if across_heads:
                for u, st in zip(starts, finals):
                    st_ref[heads[u]] = st
            else:
                st_ref[heads[0]] = finals[-1]

        n_saved = unroll if across_heads else 1
        entering = [st_ref[heads[u]] for u in range(n_saved)]
        for u in range(n_saved):
            stb_ref[u] = entering[u]
        t_idx = lax.broadcasted_iota(jnp.int32, (span * chunk, span * chunk), 0)
        s_idx = lax.broadcasted_iota(jnp.int32, (span * chunk, span * chunk), 1)
        keep = s_idx <= t_idx
        for c in range(1, span):
            keep = keep & ((s_idx >= c * chunk) | (t_idx < c * chunk))

        def fast_scores(u):
            def fn(q_ins, ks, bs):
                k_rel = jnp.concatenate([(k * jnp.exp2(-b)).astype(BF16) for k, b in zip(ks, bs)],
                                        axis=0)
                sc = lax.dot_general(jnp.concatenate(q_ins, axis=0), k_rel,
                                     (((1,), (1,)), ((), ())), preferred_element_type=F32)
                return jnp.where(keep, sc, 0.0)
            return fn
        run(fast_scores, lambda u: entering[u])

        @pl.when(jnp.logical_not(safe))
        def _():
            sub_c = lax.broadcasted_iota(jnp.int32, (SUBLANES, chunk), 0)
            col_c = lax.broadcasted_iota(jnp.int32, (SUBLANES, chunk), 1)
            consts = (col_c - sub_c, col_c)

            def exact_scores(u):
                return lambda q_ins, ks, bs: _block_diag(
                    [_hgrn_scores_exact(qkb_ref.at[u + c], consts, chunk=chunk) for c in range(span)])
            run(exact_scores, lambda u: stb_ref[u])
        return carry

    lax.fori_loop(0, n_iter if across_heads else heads_per_step * n_iter, body, 0)

    @pl.when(tt == pl.num_programs(2) - 1)
    def _():
        for hh in range(heads_per_step):
            s_out_ref[0, hh] = st_ref[hh].T


def _hgrn(proj, lbp, ng, s0, *, layer, batch, seq, tc, chunk, heads_per_step):
    m = proj.shape[1]
    nt = seq // tc
    hb = heads_per_step
    n_chunks = tc // chunk
    unroll = min(HGRN_UNROLL, n_chunks) if n_chunks > 1 else min(HGRN_HEADS_UNROLL, hb)

    def sec(cb):
        return pl.BlockSpec((hb, tc, LANES), lambda b, h, t: (cb // hb + h, b * nt + t, 0))

    return pl.pallas_call(
        functools.partial(_hgrn_kernel, layer=layer, chunk=chunk, n_chunks=tc // chunk,
                          heads_per_step=hb, unroll=unroll),
        grid=(batch, HEADS // hb, nt),
        in_specs=[
            pl.BlockSpec((hb,) + lbp.shape[1:], lambda b, h, t: (h, 0, 0)),
            pl.BlockSpec((hb, 1, LANES), lambda b, h, t: (h, 0, 0)),
            sec(CB_Q), sec(CB_F), sec(CB_V), sec(CB_GH),
            pl.BlockSpec((None, 1, hb, HEAD_DIM, HEAD_DIM), lambda b, h, t: (layer, b, h, 0, 0)),
        ],
        out_specs=[
            pl.BlockSpec((hb, tc, LANES), lambda b, h, t: (h, b * nt + t, 0)),
            pl.BlockSpec((1, hb, HEAD_DIM, HEAD_DIM), lambda b, h, t: (b, h, 0, 0)),
        ],
        out_shape=[
            jax.ShapeDtypeStruct((HEADS, m, LANES), BF16),
            jax.ShapeDtypeStruct((batch, HEADS, HEAD_DIM, HEAD_DIM), F32),
        ],
        scratch_shapes=[pltpu.VMEM((hb, HEAD_DIM, HEAD_DIM), F32),
                        pltpu.VMEM((unroll, 3, SUBLANES + chunk, LANES), F32),
                        pltpu.VMEM((unroll if n_chunks == 1 else 1, HEAD_DIM, HEAD_DIM), F32)],
        compiler_params=pltpu.CompilerParams(
            dimension_semantics=("parallel", "parallel", "arbitrary"),
            vmem_limit_bytes=VMEM_LIMIT_BYTES),
        name="hgrn2",
    )(lbp, ng, proj, proj, proj, proj, s0)


def _rglru_kernel(xr_ref, gr_ref, hist_ref, h0_ref, cw_ref, cb_ref, wa_ref, ba_ref, wx_ref, bx_ref,
                  lam_ref, y_ref, conv_out_ref, h_out_ref, xbuf_ref, hcar_ref, a_ref, b_ref, p_ref, l_ref,
                  *, tc, blocks_per_step, stream_start, n_vregs):
    tt = pl.program_id(2)
    last = tt == pl.num_programs(2) - 1
    hist_lo = SUBLANES - (CONV_W - 1)
    n_seg = SUBLANES * n_vregs
    seg = tc // n_seg
    pitch = seg + SCAN_PAD
    sub = lax.broadcasted_iota(jnp.int32, (SUBLANES, LANES), 0)

    @pl.when(tt == 0)
    def _():
        for jj in range(blocks_per_step):
            xbuf_ref[jj, hist_lo:SUBLANES, :] = hist_ref[0, jj]
            hcar_ref[jj] = jnp.broadcast_to(h0_ref[0, jj], (SUBLANES, LANES))

    def block_body(jj):
        x = xr_ref[jj]
        xbuf_ref[jj, SUBLANES:SUBLANES + tc, :] = x
        cw = cw_ref[jj]
        xc = cb_ref[jj] + xbuf_ref[jj, pl.ds(hist_lo, tc), :] * cw[0:1, :]
        for j in range(1, CONV_W):
            xc = xc + xbuf_ref[jj, pl.ds(hist_lo + j, tc), :] * cw[j:j + 1, :]
        tail = x[tc - (CONV_W - 1):tc, :]
        xbuf_ref[jj, hist_lo:SUBLANES, :] = tail

        xcb = xc.astype(BF16)
        r = jax.nn.sigmoid(jnp.dot(xcb, wa_ref[jj], preferred_element_type=F32) + ba_ref[jj])
        gi = jax.nn.sigmoid(jnp.dot(xcb, wx_ref[jj], preferred_element_type=F32) + bx_ref[jj])
        nl = -lam_ref[jj]
        rate = -RG_C * (jnp.maximum(nl, 0.0) + jnp.log1p(jnp.exp(-jnp.abs(nl))))
        log_a = r * rate
        a = jnp.exp2(r * (rate * LOG2E))
        m2 = -jnp.tanh(log_a) * (a * a + 1.0)
        mult = jnp.where(m2 > 0.0, m2 * lax.rsqrt(m2), 0.0)
        if stream_start:
            first = jnp.where((sub == 0) & (tt == 0), 1.0, mult[0:SUBLANES])
            mult = jnp.concatenate([first, mult[SUBLANES:]], axis=0)
        bt = mult * gi * xc
        for s in range(n_seg):
            a_ref[jj, s * pitch:s * pitch + seg, :] = a[s * seg:(s + 1) * seg]
            b_ref[jj, s * pitch:s * pitch + seg, :] = bt[s * seg:(s + 1) * seg]

        def scan_body(i, pl_prev):
            out = []
            for v, (p_prev, l_prev) in enumerate(pl_prev):
                rows = pl.ds(i + v * SUBLANES * pitch, SUBLANES, stride=pitch)
                ai = a_ref.at[jj][rows, :]
                l_new = ai * l_prev + b_ref.at[jj][rows, :]
                p_new = ai * p_prev
                p_ref.at[jj][rows, :] = p_new
                l_ref.at[jj][rows, :] = l_new
                out.append((p_new, l_new))
            return tuple(out)

        init = tuple((jnp.ones((SUBLANES, LANES), F32), jnp.zeros((SUBLANES, LANES), F32))
                     for _ in range(n_vregs))
        ends = lax.fori_loop(0, seg, scan_body, init, unroll=min(8, seg))
        h_in = hcar_ref[jj]
        pieces = []
        for v, (p_end, l_end) in enumerate(ends):
            for d in (1, 2, 4):
                keep = sub >= d
                l_end = p_end * jnp.where(keep, pltpu.roll(l_end, d, 0), 0.0) + l_end
                p_end = p_end * jnp.where(keep, pltpu.roll(p_end, d, 0), 1.0)
            h_seg_end = p_end * h_in + l_end
            h_seg_start = jnp.where(sub >= 1, pltpu.roll(h_seg_end, 1, 0), h_in)
            h_in = jnp.broadcast_to(h_seg_end[SUBLANES - 1:SUBLANES, :], (SUBLANES, LANES))
            for r in range(SUBLANES):
                lo = (v * SUBLANES + r) * pitch
                pieces.append(p_ref[jj, lo:lo + seg, :] * h_seg_start[r:r + 1, :]
                              + l_ref[jj, lo:lo + seg, :])
        h_last = h_in
        hcar_ref[jj] = h_last
        h = jnp.concatenate(pieces, axis=0)
        y_ref[jj] = (h * _silu(gr_ref[jj])).astype(BF16)

        @pl.when(last)
        def _():
            conv_out_ref[0, jj] = tail
            h_out_ref[0, jj] = h_last[0:1, :]

    for jj in range(blocks_per_step):
        block_body(jj)


def _rglru(proj, hist, h0, cw, cb, wa, ba, wx, bx, lam, *, batch, seq, tc, blocks_per_step,
           stream_start):
    m = proj.shape[1]
    nt = seq // tc
    jb = blocks_per_step
    n_vregs = min(SCAN_VREGS, tc // SUBLANES)

    def sec(cb_off):
        return pl.BlockSpec((jb, tc, LANES), lambda b, j, t: (cb_off // jb + j, b * nt + t, 0))

    def per_block(shape):
        return pl.BlockSpec((jb,) + shape, lambda b, j, t: (j,) + (0,) * len(shape))

    return pl.pallas_call(
        functools.partial(_rglru_kernel, tc=tc, blocks_per_step=jb, stream_start=stream_start,
                          n_vregs=n_vregs),
        grid=(batch, RG_BLOCKS // jb, nt),
        in_specs=[
            sec(CB_XR), sec(CB_GR),
            pl.BlockSpec((1, jb, CONV_W - 1, LANES), lambda b, j, t: (b, j, 0, 0)),
            pl.BlockSpec((1, jb, 1, LANES), lambda b, j, t: (b, j, 0, 0)),
            per_block((CONV_W, LANES)), per_block((1, LANES)),
            per_block((LANES, LANES)), per_block((1, LANES)),
            per_block((LANES, LANES)), per_block((1, LANES)),
            per_block((1, LANES)),
        ],
        out_specs=[
            pl.BlockSpec((jb, tc, LANES), lambda b, j, t: (j, b * nt + t, 0)),
            pl.BlockSpec((1, jb, CONV_W - 1, LANES), lambda b, j, t: (b, j, 0, 0)),
            pl.BlockSpec((1, jb, 1, LANES), lambda b, j, t: (b, j, 0, 0)),
        ],
        out_shape=[
            jax.ShapeDtypeStruct((RG_BLOCKS, m, LANES), BF16),
            jax.ShapeDtypeStruct((batch, RG_BLOCKS, CONV_W - 1, LANES), F32),
            jax.ShapeDtypeStruct((batch, RG_BLOCKS, 1, LANES), F32),
        ],
        scratch_shapes=[
            pltpu.VMEM((jb, SUBLANES + tc, LANES), F32),
            pltpu.VMEM((jb, SUBLANES, LANES), F32),
        ] + [pltpu.VMEM((jb, tc + SUBLANES * n_vregs * SCAN_PAD, LANES), F32)] * 4,
        compiler_params=pltpu.CompilerParams(
            dimension_semantics=("parallel", "parallel", "arbitrary"),
            vmem_limit_bytes=VMEM_LIMIT_BYTES),
        name="rglru",
    )(proj, proj, hist, h0, cw, cb, wa, ba, wx, bx, lam)


def _out_kernel(yh_ref, yr_ref, mh_ref, mr_ref, x_ref, wbh_ref, wbr_ref, wo_ref, g_ref, o_ref):
    def gather(ref):
        return jnp.concatenate([ref[c] for c in range(D_MODEL // LANES)], axis=1)

    b_h = jnp.dot(gather(yh_ref), wbh_ref[...], preferred_element_type=F32)
    b_r = jnp.dot(gather(yr_ref), wbr_ref[...], preferred_element_type=F32)
    merged = jax.nn.sigmoid(gather(mh_ref)) * b_h + jax.nn.sigmoid(gather(mr_ref)) * b_r
    out = jnp.dot(merged.astype(BF16), wo_ref[...], preferred_element_type=F32)
    ms = jnp.mean(out * out, axis=-1, keepdims=True)
    o_ref[...] = x_ref[...] + out * lax.rsqrt(ms + EPS) * g_ref[...]


def _out_stage(yh, yr, proj, x, wbh, wbr, wo, g, *, layer, tm):
    m = x.shape[0]
    nb = D_MODEL // LANES

    def resident(shape):
        return pl.BlockSpec(shape, lambda i: (0,) * len(shape), pipeline_mode=pl.Buffered(1))

    stacked_weight = pl.BlockSpec((None, D_MODEL, D_MODEL), lambda i: (layer, 0, 0),
                                  pipeline_mode=pl.Buffered(1))

    return pl.pallas_call(
        _out_kernel,
        grid=(m // tm,),
        in_specs=[
            pl.BlockSpec((nb, tm, LANES), lambda i: (0, i, 0)),
            pl.BlockSpec((nb, tm, LANES), lambda i: (0, i, 0)),
            pl.BlockSpec((nb, tm, LANES), lambda i: (CB_MH // nb, i, 0)),
            pl.BlockSpec((nb, tm, LANES), lambda i: (CB_MR // nb, i, 0)),
            pl.BlockSpec((tm, D_MODEL), lambda i: (i, 0)),
            stacked_weight, stacked_weight, stacked_weight,
            resident((1, D_MODEL)),
        ],
        out_specs=pl.BlockSpec((tm, D_MODEL), lambda i: (i, 0)),
        out_shape=jax.ShapeDtypeStruct((m, D_MODEL), F32),
        compiler_params=pltpu.CompilerParams(
            dimension_semantics=("parallel",), vmem_limit_bytes=VMEM_LIMIT_BYTES),
        name="out_stage",
    )(yh, yr, proj, proj, x, wbh, wbr, wo, g)


def _stream_tiles(batch, seq):
    m = batch * seq
    tm_in = min(m, 1024)
    tn_in = 2048
    tm_out = min(m, 256)
    tc = min(seq, 8192)
    chunk = min(seq, HGRN_CHUNK)
    per_step = 1 if seq >= 1024 else HEADS
    return dict(tm_in=tm_in, tn_in=tn_in, tm_out=tm_out, tc=tc, chunk=chunk, per_step=per_step)


def _run_trunk(x, s_hgrn, conv_hist, h_rg, stream_start, w):
    batch, seq, _ = x.shape
    m = batch * seq
    t = _stream_tiles(batch, seq)
    xf = x.reshape(m, D_MODEL)
    new_s, new_conv, new_h = [], [], []
    for layer in range(w["w_in"].shape[0]):
        proj = _in_proj(xf, w["pre_norm_g"][layer], w["w_in"], layer=layer, tm=t["tm_in"], tn=t["tn_in"])
        yh, s_l = _hgrn(proj, w["lbp"], w["hgrn_norm_g"][layer], s_hgrn, layer=layer,
                        batch=batch, seq=seq, tc=t["tc"], chunk=t["chunk"],
                        heads_per_step=t["per_step"])
        hist = conv_hist[layer].reshape(batch, CONV_W - 1, RG_BLOCKS, LANES).transpose(0, 2, 1, 3)
        h0 = h_rg[layer].reshape(batch, RG_BLOCKS, 1, LANES)
        yr, conv_l, h_l = _rglru(
            proj, hist, h0, w["conv_w"][layer], w["conv_b"][layer], w["w_gate_a"][layer],
            w["b_gate_a"][layer], w["w_gate_x"][layer], w["b_gate_x"][layer], w["rg_lambda"][layer],
            batch=batch, seq=seq, tc=t["tc"], blocks_per_step=t["per_step"],
            stream_start=stream_start)
        xf = _out_stage(yh, yr, proj, xf, w["w_branch_hgrn"], w["w_branch_rglru"], w["w_out"],
                        w["post_norm_g"][layer], layer=layer, tm=t["tm_out"])
        new_s.append(s_l)
        new_conv.append(conv_l.transpose(0, 2, 1, 3).reshape(batch, CONV_W - 1, D_MODEL))
        new_h.append(h_l.reshape(batch, D_MODEL))
    return xf.reshape(batch, seq, D_MODEL), jnp.stack(new_s), jnp.stack(new_conv), jnp.stack(new_h)


def _prep_weights(lb_param, pre_norm_g, w_in, hgrn_norm_g, w_branch_hgrn, conv_w, conv_b, w_gate_a,
                  b_gate_a, w_gate_x, b_gate_x, rg_lambda, w_branch_rglru, w_out, post_norm_g):
    depth = w_in.shape[0]

    def rows(a):
        return a.reshape(depth, 1, D_MODEL)

    def blocks(a):
        return a.reshape(depth, RG_BLOCKS, 1, LANES)

    return dict(
        lbp=lb_param.astype(F32).reshape(depth, HEADS, LANES).transpose(1, 0, 2),
        pre_norm_g=rows(pre_norm_g), post_norm_g=rows(post_norm_g),
        hgrn_norm_g=hgrn_norm_g.reshape(depth, HEADS, 1, LANES),
        w_in=w_in.astype(BF16), w_branch_hgrn=w_branch_hgrn.astype(BF16),
        w_branch_rglru=w_branch_rglru.astype(BF16), w_out=w_out.astype(BF16),
        conv_w=conv_w.reshape(depth, CONV_W, RG_BLOCKS, LANES).transpose(0, 2, 1, 3),
        conv_b=blocks(conv_b), w_gate_a=w_gate_a.astype(BF16), b_gate_a=blocks(b_gate_a),
        w_gate_x=w_gate_x.astype(BF16), b_gate_x=blocks(b_gate_x), rg_lambda=blocks(rg_lambda),
    )


def kernel(x_prompt, x_sample, state_hgrn, state_conv, state_rglru, lb_param, pre_norm_g, w_in,
           hgrn_norm_g, w_branch_hgrn, conv_w, conv_b, w_gate_a, b_gate_a, w_gate_x, b_gate_x,
           rg_lambda, w_branch_rglru, w_out, post_norm_g):
    w = _prep_weights(lb_param, pre_norm_g, w_in, hgrn_norm_g, w_branch_hgrn, conv_w, conv_b,
                      w_gate_a, b_gate_a, w_gate_x, b_gate_x, rg_lambda, w_branch_rglru, w_out,
                      post_norm_g)
    depth = w_in.shape[0]
    nb = x_prompt.shape[0]
    s0 = jnp.zeros((depth, nb, HEADS, HEAD_DIM, HEAD_DIM), F32)
    c0 = jnp.zeros((depth, nb, CONV_W - 1, D_MODEL), F32)
    h0 = jnp.zeros((depth, nb, D_MODEL), F32)
    y_p, s_p, c_p, h_p = _run_trunk(x_prompt, s0, c0, h0, True, w)
    y_s, s_s, c_s, h_s = _run_trunk(x_sample, state_hgrn, state_conv, state_rglru, False, w)
    return (y_p, y_s, s_p, c_p, h_p, s_s, c_s, h_s)
```
